```python
import jax, jax.numpy as jnp
from jax import lax
import numpy as np

D_MODEL = 2048
BATCH = 1
SEQ = 8192
DEPTH = 4

HEAD_DIM = 128
CONV_CH = D_MODEL // 2
CONV_WIDTH = 31
MOBA_HEADS = (D_MODEL // 2) // HEAD_DIM
MOBA_DIM = MOBA_HEADS * HEAD_DIM
MOBA_BLOCK = 256
MOBA_TOPK = 3
EVEN_IN = 2 * CONV_CH + 3 * MOBA_DIM
NSA_HEADS = D_MODEL // HEAD_DIM
NSA_KV_HEADS = NSA_HEADS // 4
NSA_GROUP = NSA_HEADS // NSA_KV_HEADS
NSA_DIM = NSA_HEADS * HEAD_DIM
NSA_KV_DIM = NSA_KV_HEADS * HEAD_DIM
CMP_LEN = 32
CMP_STRIDE = 16
CMP_HIDDEN = 256
SLC_BLOCK = 64
SLC_TOPN = 16
WINDOW = 512
ODD_IN = NSA_DIM + 6 * NSA_KV_DIM + 3 * NSA_HEADS
D_FF = ((8 * D_MODEL) // 3 + 255) // 256 * 256
N_EXPERTS = 8
TOP_K = 2
Q_CHUNK = 64
EPS = 1e-6
NEG = -1e30

kernel_name = 'hybrid_conv_moba_nsa_moe_decoder'


def rms_norm(x, g):
    xf = x.astype(jnp.float32)
    y = xf * lax.rsqrt(jnp.mean(xf * xf, axis=-1, keepdims=True) + EPS)
    return (y * g.astype(jnp.float32)).astype(x.dtype)


def layer_norm(x, g, b):
    xf = x.astype(jnp.float32)
    mu = jnp.mean(xf, axis=-1, keepdims=True)
    var = jnp.mean(jnp.square(xf - mu), axis=-1, keepdims=True)
    y = (xf - mu) * lax.rsqrt(var + EPS)
    return (y * g.astype(jnp.float32) + b.astype(jnp.float32)).astype(x.dtype)


def alibi_slopes(n):
    return jnp.asarray(2.0 ** (-8.0 * np.arange(1, n + 1) / n), dtype=jnp.float32)


def split_cols(u, sizes):
    return jnp.split(u, np.cumsum(sizes)[:-1].tolist(), axis=-1)


def to_heads(t, n):
    b, s, _ = t.shape
    return t.reshape(b, s, n, HEAD_DIM).transpose(0, 2, 1, 3)


def from_heads(t):
    b, h, s, d = t.shape
    return t.transpose(0, 2, 1, 3).reshape(b, s, h * d)


def swiglu(h, w_gate, w_up, w_down):
    return (jax.nn.silu(h @ w_gate) * (h @ w_up)) @ w_down


def moba_attention(q, k, v, slopes):
    b, nh, t, dh = q.shape
    nb = -(-t // MOBA_BLOCK)
    pad = nb * MOBA_BLOCK - t
    kp = jnp.pad(k, ((0, 0), (0, 0), (0, pad), (0, 0)))
    vp = jnp.pad(v, ((0, 0), (0, 0), (0, pad), (0, 0)))
    kb = kp.reshape(b, nh, nb, MOBA_BLOCK, dh)
    vb = vp.reshape(b, nh, nb, MOBA_BLOCK, dh)
    kmean = jnp.mean(kb.astype(jnp.float32), axis=3)
    n_sel = min(MOBA_TOPK, nb)
    scale = dh ** -0.5
    bi = jnp.arange(b)[:, None, None, None]
    hi = jnp.arange(nh)[None, :, None, None]
    sl4 = slopes[None, :, None, None]
    sl5 = slopes[None, :, None, None, None]
    blk_ids = jnp.arange(nb)
    offs = jnp.arange(MOBA_BLOCK)

    def chunk(c0):
        tq = c0 + jnp.arange(Q_CHUNK)
        qc = lax.dynamic_slice_in_dim(q, c0, Q_CHUNK, axis=2)
        qb = c0 // MOBA_BLOCK
        gate = jnp.einsum('bhqd,bhnd->bhqn', qc.astype(jnp.float32), kmean)
        gate = jnp.where(blk_ids < qb, gate, -jnp.inf)
        _, idx = lax.top_k(gate, n_sel)
        sel_ok = idx < qb
        kg = kb[bi, hi, idx]
        vg = vb[bi, hi, idx]
        s_g = jnp.einsum('bhqd,bhqjkd->bhqjk', qc, kg).astype(jnp.float32) * scale
        pos_g = idx[..., None] * MOBA_BLOCK + offs
        dist_g = (tq[None, None, :, None, None] - pos_g).astype(jnp.float32)
        s_g = jnp.where(sel_ok[..., None], s_g - sl5 * dist_g, NEG)
        s_g = s_g.reshape(b, nh, Q_CHUNK, n_sel * MOBA_BLOCK)
        ko = lax.dynamic_slice_in_dim(kp, qb * MOBA_BLOCK, MOBA_BLOCK, axis=2)
        vo = lax.dynamic_slice_in_dim(vp, qb * MOBA_BLOCK, MOBA_BLOCK, axis=2)
        dist_o = tq[:, None] - (qb * MOBA_BLOCK + offs)[None, :]
        s_o = jnp.einsum('bhqd,bhkd->bhqk', qc, ko).astype(jnp.float32) * scale
        s_o = jnp.where(dist_o >= 0, s_o - sl4 * dist_o.astype(jnp.float32), NEG)
        p = jax.nn.softmax(jnp.concatenate([s_g, s_o], axis=-1), axis=-1).astype(v.dtype)
        p_g, p_o = p[..., :n_sel * MOBA_BLOCK], p[..., n_sel * MOBA_BLOCK:]
        o = jnp.einsum('bhqm,bhqmd->bhqd', p_g, vg.reshape(b, nh, Q_CHUNK, n_sel * MOBA_BLOCK, dh))
        return o + jnp.einsum('bhqk,bhkd->bhqd', p_o, vo)

    out = lax.map(chunk, jnp.arange(t // Q_CHUNK, dtype=jnp.int32) * Q_CHUNK)
    return jnp.moveaxis(out, 0, 2).reshape(b, nh, t, dh)


def compress(t_in, pe, w1, w2):
    b, g, t, dh = t_in.shape
    nc = (t - CMP_LEN) // CMP_STRIDE + 1
    idx = np.arange(nc)[:, None] * CMP_STRIDE + np.arange(CMP_LEN)[None, :]
    blocks = t_in[:, :, idx] + pe
    flat = blocks.reshape(b, g, nc, CMP_LEN * dh)
    return jax.nn.silu(flat @ w1) @ w2


def cmp_slc_overlap(nc, ns):
    start = np.arange(nc)[:, None] * CMP_STRIDE
    lo = np.arange(ns)[None, :] * SLC_BLOCK
    return ((start < lo + SLC_BLOCK) & (start + CMP_LEN > lo)).astype(np.float32)


def nsa_attention(q, k_cmp, v_cmp, k_slc, v_slc, k_win, v_win, gates, slopes):
    b, g, r, t, dh = q.shape
    nc = k_cmp.shape[2]
    ns = t // SLC_BLOCK
    n_sel = min(SLC_TOPN, ns)
    scale = dh ** -0.5
    cmp_end = jnp.asarray(np.arange(nc) * CMP_STRIDE + CMP_LEN - 1, dtype=jnp.int32)
    overlap = jnp.asarray(cmp_slc_overlap(nc, ns))
    ksb = k_slc.reshape(b, g, ns, SLC_BLOCK, dh)
    vsb = v_slc.reshape(b, g, ns, SLC_BLOCK, dh)
    kwp = jnp.pad(k_win, ((0, 0), (0, 0), (WINDOW, 0), (0, 0)))
    vwp = jnp.pad(v_win, ((0, 0), (0, 0), (WINDOW, 0), (0, 0)))
    bi = jnp.arange(b)[:, None, None, None]
    gi = jnp.arange(g)[None, :, None, None]
    sl5 = slopes[None, :, :, None, None]
    sl6 = slopes[None, :, :, None, None, None]
    blk_ids = jnp.arange(ns)
    offs = jnp.arange(SLC_BLOCK)

    def chunk(c0):
        tq = c0 + jnp.arange(Q_CHUNK)
        qc = lax.dynamic_slice_in_dim(q, c0, Q_CHUNK, axis=3)
        gc = lax.dynamic_slice_in_dim(gates, c0, Q_CHUNK, axis=3)
        dist_c = tq[:, None] - cmp_end[None, :]
        ok_c = dist_c >= 0
        s_c = jnp.einsum('bgrqd,bgnd->bgrqn', qc, k_cmp).astype(jnp.float32) * scale
        s_c = jnp.where(ok_c, s_c - sl5 * dist_c.astype(jnp.float32), NEG)
        p_c = jax.nn.softmax(s_c, axis=-1) * jnp.any(ok_c, axis=-1)[:, None].astype(jnp.float32)
        o_c = jnp.einsum('bgrqn,bgnd->bgrqd', p_c.astype(v_cmp.dtype), v_cmp)
        imp = jnp.einsum('bgrqn,ns->bgqs', p_c, overlap)
        qb = c0 // SLC_BLOCK
        forced = (blk_ids == 0) | (blk_ids == qb) | (blk_ids == qb - 1)
        imp = jnp.where(forced, jnp.inf, jnp.where(blk_ids <= qb, imp, -jnp.inf))
        _, idx = lax.top_k(imp, n_sel)
        kg = ksb[bi, gi, idx]
        vg = vsb[bi, gi, idx]
        s_s = jnp.einsum('bgrqd,bgqjkd->bgrqjk', qc, kg).astype(jnp.float32) * scale
        dist_s = tq[None, None, :, None, None] - (idx[..., None] * SLC_BLOCK + offs)
        dist_s = dist_s[:, :, None]
        s_s = jnp.where(dist_s >= 0, s_s - sl6 * dist_s.astype(jnp.float32), NEG)
        p_s = jax.nn.softmax(s_s.reshape(b, g, r, Q_CHUNK, n_sel * SLC_BLOCK), axis=-1)
        o_s = jnp.einsum('bgrqm,bgqmd->bgrqd', p_s.astype(v_slc.dtype), vg.reshape(b, g, Q_CHUNK, n_sel * SLC_BLOCK, dh))
        kwc = lax.dynamic_slice_in_dim(kwp, c0, WINDOW + Q_CHUNK, axis=2)
        vwc = lax.dynamic_slice_in_dim(vwp, c0, WINDOW + Q_CHUNK, axis=2)
        pos_w = c0 - WINDOW + jnp.arange(WINDOW + Q_CHUNK)
        dist_w = tq[:, None] - pos_w[None, :]
        ok_w = (pos_w >= 0)[None, :] & (dist_w >= 0) & (dist_w < WINDOW)
        s_w = jnp.einsum('bgrqd,bgkd->bgrqk', qc, kwc).astype(jnp.float32) * scale
        s_w = jnp.where(ok_w, s_w - sl5 * dist_w.astype(jnp.float32), NEG)
        p_w = jax.nn.softmax(s_w, axis=-1)
        o_w = jnp.einsum('bgrqk,bgkd->bgrqd', p_w.astype(v_win.dtype), vwc)
        o = gc[..., 0:1] * o_c + gc[..., 1:2] * o_s + gc[..., 2:3] * o_w
        return o.astype(q.dtype)

    out = lax.map(chunk, jnp.arange(t // Q_CHUNK, dtype=jnp.int32) * Q_CHUNK)
    return jnp.moveaxis(out, 0, 3).reshape(b, g, r, t, dh)


def moe_swiglu(h, w_router, w_gate, w_up, w_down):
    logits = (h @ w_router).astype(jnp.float32)
    top_val, top_idx = lax.top_k(logits, TOP_K)
    top_w = jax.nn.softmax(top_val, axis=-1)
    combine = jnp.sum(jax.nn.one_hot(top_idx, N_EXPERTS, dtype=jnp.float32) * top_w[..., None], axis=-2).astype(h.dtype)
    out = jnp.zeros_like(h)
    for e in range(N_EXPERTS):
        out = out + combine[..., e:e + 1] * swiglu(h, w_gate[e], w_up[e], w_down[e])
    return out


def conv_moba_layer(x, attn_norm, w_in, conv_w, conv_b, conv_norm_g, conv_norm_b, q_norm, k_norm, w_out, ffn_norm, w_gate, w_up, w_down):
    h = rms_norm(x, attn_norm)
    u = h @ w_in
    a_val, a_gate, q, k, v = split_cols(u, [CONV_CH, CONV_CH, MOBA_DIM, MOBA_DIM, MOBA_DIM])
    a = a_val * jax.nn.sigmoid(a_gate)
    a = lax.conv_general_dilated(a, conv_w, (1,), [(CONV_WIDTH - 1, 0)], dimension_numbers=('NWC', 'WIO', 'NWC'), feature_group_count=CONV_CH) + conv_b
    a = jax.nn.silu(layer_norm(a, conv_norm_g, conv_norm_b))
    q = rms_norm(to_heads(q, MOBA_HEADS), q_norm)
    k = rms_norm(to_heads(k, MOBA_HEADS), k_norm)
    o = moba_attention(q, k, to_heads(v, MOBA_HEADS), alibi_slopes(MOBA_HEADS))
    x = x + jnp.concatenate([a, from_heads(o.astype(x.dtype))], axis=-1) @ w_out
    return x + swiglu(rms_norm(x, ffn_norm), w_gate, w_up, w_down)


def nsa_moe_layer(x, attn_norm, w_in, q_norm, kc_norm, ks_norm, kw_norm, cmp_pe_k, cmp_w1_k, cmp_w2_k, cmp_pe_v, cmp_w1_v, cmp_w2_v, w_out, ffn_norm, w_router, w_gate, w_up, w_down):
    b, t, _ = x.shape
    h = rms_norm(x, attn_norm)
    u = h @ w_in
    q, kc, vc, ks, vs, kw, vw, g = split_cols(u, [NSA_DIM] + [NSA_KV_DIM] * 6 + [3 * NSA_HEADS])
    q = rms_norm(to_heads(q, NSA_HEADS), q_norm).reshape(b, NSA_KV_HEADS, NSA_GROUP, t, HEAD_DIM)
    k_cmp = rms_norm(compress(to_heads(kc, NSA_KV_HEADS), cmp_pe_k, cmp_w1_k, cmp_w2_k), kc_norm)
    v_cmp = compress(to_heads(vc, NSA_KV_HEADS), cmp_pe_v, cmp_w1_v, cmp_w2_v)
    k_slc = rms_norm(to_heads(ks, NSA_KV_HEADS), ks_norm)
    k_win = rms_norm(to_heads(kw, NSA_KV_HEADS), kw_norm)
    gates = jax.nn.sigmoid(g.astype(jnp.float32)).reshape(b, t, NSA_HEADS, 3).transpose(0, 2, 1, 3).reshape(b, NSA_KV_HEADS, NSA_GROUP, t, 3)
    o = nsa_attention(q, k_cmp, v_cmp, k_slc, to_heads(vs, NSA_KV_HEADS), k_win, to_heads(vw, NSA_KV_HEADS), gates, alibi_slopes(NSA_HEADS).reshape(NSA_KV_HEADS, NSA_GROUP))
    o = o.reshape(b, NSA_HEADS, t, HEAD_DIM).astype(x.dtype)
    x = x + from_heads(o) @ w_out
    return x + moe_swiglu(rms_norm(x, ffn_norm), w_router, w_gate, w_up, w_down)


def setup_inputs(seed: int = 0) -> dict:
    key = jax.random.key(seed)
    key, kx = jax.random.split(key)
    inputs = {'x': jax.random.normal(kx, (BATCH, SEQ, D_MODEL), jnp.float32)}
    for i in range(DEPTH):
        key, lk = jax.random.split(key)
        keys = list(jax.random.split(lk, 20))

        def w(shape, fan_in):
            return jax.random.normal(keys.pop(), shape, jnp.float32) * fan_in ** -0.5

        def gain(n):
            return 1.0 + 0.05 * jax.random.normal(keys.pop(), (n,), jnp.float32)

        def small(shape, s=0.02):
            return s * jax.random.normal(keys.pop(), shape, jnp.float32)

        pre = f'l{i}_'
        if i % 2 == 0:
            inputs[pre + 'attn_norm'] = gain(D_MODEL)
            inputs[pre + 'w_in'] = w((D_MODEL, EVEN_IN), D_MODEL)
            inputs[pre + 'conv_w'] = w((CONV_WIDTH, 1, CONV_CH), CONV_WIDTH)
            inputs[pre + 'conv_b'] = small((CONV_CH,))
            inputs[pre + 'conv_norm_g'] = gain(CONV_CH)
            inputs[pre + 'conv_norm_b'] = small((CONV_CH,))
            inputs[pre + 'q_norm'] = gain(HEAD_DIM)
            inputs[pre + 'k_norm'] = gain(HEAD_DIM)
            inputs[pre + 'w_out'] = w((CONV_CH + MOBA_DIM, D_MODEL), CONV_CH + MOBA_DIM)
            inputs[pre + 'ffn_norm'] = gain(D_MODEL)
            inputs[pre + 'w_gate'] = w((D_MODEL, D_FF), D_MODEL)
            inputs[pre + 'w_up'] = w((D_MODEL, D_FF), D_MODEL)
            inputs[pre + 'w_down'] = w((D_FF, D_MODEL), D_FF)
        else:
            inputs[pre + 'attn_norm'] = gain(D_MODEL)
            inputs[pre + 'w_in'] = w((D_MODEL, ODD_IN), D_MODEL)
            inputs[pre + 'q_norm'] = gain(HEAD_DIM)
            inputs[pre + 'kc_norm'] = gain(HEAD_DIM)
            inputs[pre + 'ks_norm'] = gain(HEAD_DIM)
            inputs[pre + 'kw_norm'] = gain(HEAD_DIM)
            inputs[pre + 'cmp_pe_k'] = small((CMP_LEN, HEAD_DIM), 0.1)
            inputs[pre + 'cmp_w1_k'] = w((CMP_LEN * HEAD_DIM, CMP_HIDDEN), CMP_LEN * HEAD_DIM)
            inputs[pre + 'cmp_w2_k'] = w((CMP_HIDDEN, HEAD_DIM), CMP_HIDDEN)
            inputs[pre + 'cmp_pe_v'] = small((CMP_LEN, HEAD_DIM), 0.1)
            inputs[pre + 'cmp_w1_v'] = w((CMP_LEN * HEAD_DIM, CMP_HIDDEN), CMP_LEN * HEAD_DIM)
            inputs[pre + 'cmp_w2_v'] = w((CMP_HIDDEN, HEAD_DIM), CMP_HIDDEN)
            inputs[pre + 'w_out'] = w((NSA_DIM, D_MODEL), NSA_DIM)
            inputs[pre + 'ffn_norm'] = gain(D_MODEL)
            inputs[pre + 'w_router'] = w((D_MODEL, N_EXPERTS), D_MODEL)
            inputs[pre + 'w_gate'] = w((N_EXPERTS, D_MODEL, D_FF), D_MODEL)
            inputs[pre + 'w_up'] = w((N_EXPERTS, D_MODEL, D_FF), D_MODEL)
            inputs[pre + 'w_down'] = w((N_EXPERTS, D_FF, D_MODEL), D_FF)
    return inputs


def reference(x,
              l0_attn_norm, l0_w_in, l0_conv_w, l0_conv_b, l0_conv_norm_g, l0_conv_norm_b, l0_q_norm, l0_k_norm, l0_w_out, l0_ffn_norm, l0_w_gate, l0_w_up, l0_w_down,
              l1_attn_norm, l1_w_in, l1_q_norm, l1_kc_norm, l1_ks_norm, l1_kw_norm, l1_cmp_pe_k, l1_cmp_w1_k, l1_cmp_w2_k, l1_cmp_pe_v, l1_cmp_w1_v, l1_cmp_w2_v, l1_w_out, l1_ffn_norm, l1_w_router, l1_w_gate, l1_w_up, l1_w_down,
              l2_attn_norm, l2_w_in, l2_conv_w, l2_conv_b, l2_conv_norm_g, l2_conv_norm_b, l2_q_norm, l2_k_norm, l2_w_out, l2_ffn_norm, l2_w_gate, l2_w_up, l2_w_down,
              l3_attn_norm, l3_w_in, l3_q_norm, l3_kc_norm, l3_ks_norm, l3_kw_norm, l3_cmp_pe_k, l3_cmp_w1_k, l3_cmp_w2_k, l3_cmp_pe_v, l3_cmp_w1_v, l3_cmp_w2_v, l3_w_out, l3_ffn_norm, l3_w_router, l3_w_gate, l3_w_up, l3_w_down):
    layer_params = [
        (l0_attn_norm, l0_w_in, l0_conv_w, l0_conv_b, l0_conv_norm_g, l0_conv_norm_b, l0_q_norm, l0_k_norm, l0_w_out, l0_ffn_norm, l0_w_gate, l0_w_up, l0_w_down),
        (l1_attn_norm, l1_w_in, l1_q_norm, l1_kc_norm, l1_ks_norm, l1_kw_norm, l1_cmp_pe_k, l1_cmp_w1_k, l1_cmp_w2_k, l1_cmp_pe_v, l1_cmp_w1_v, l1_cmp_w2_v, l1_w_out, l1_ffn_norm, l1_w_router, l1_w_gate, l1_w_up, l1_w_down),
        (l2_attn_norm, l2_w_in, l2_conv_w, l2_conv_b, l2_conv_norm_g, l2_conv_norm_b, l2_q_norm, l2_k_norm, l2_w_out, l2_ffn_norm, l2_w_gate, l2_w_up, l2_w_down),
        (l3_attn_norm, l3_w_in, l3_q_norm, l3_kc_norm, l3_ks_norm, l3_kw_norm, l3_cmp_pe_k, l3_cmp_w1_k, l3_cmp_w2_k, l3_cmp_pe_v, l3_cmp_w1_v, l3_cmp_w2_v, l3_w_out, l3_ffn_norm, l3_w_router, l3_w_gate, l3_w_up, l3_w_down),
    ]
    for i in range(DEPTH):
        if i % 2 == 0:
            x = conv_moba_layer(x, *layer_params[i])
        else:
            x = nsa_moe_layer(x, *layer_params[i])
    return x
```

```python
import functools

import numpy as np
import jax
import jax.numpy as jnp
from jax import lax
from jax.experimental import pallas as pl
from jax.experimental.pallas import tpu as pltpu

D_MODEL = 2048
HEAD_DIM = 128
CONV_CH = D_MODEL // 2
CONV_WIDTH = 31
MOBA_HEADS = (D_MODEL // 2) // HEAD_DIM
MOBA_DIM = MOBA_HEADS * HEAD_DIM
MOBA_BLOCK = 256
MOBA_TOPK = 3
NSA_HEADS = D_MODEL // HEAD_DIM
NSA_KV_HEADS = NSA_HEADS // 4
NSA_GROUP = NSA_HEADS // NSA_KV_HEADS
NSA_DIM = NSA_HEADS * HEAD_DIM
NSA_KV_DIM = NSA_KV_HEADS * HEAD_DIM
CMP_LEN = 32
CMP_STRIDE = 16
CMP_HIDDEN = 256
SLC_BLOCK = 64
SLC_TOPN = 16
WINDOW = 512
D_FF = ((8 * D_MODEL) // 3 + 255) // 256 * 256
N_EXPERTS = 8
TOP_K = 2
EPS = 1e-6
NEG = -1e30

LANES = 128
F32 = jnp.float32
BF16 = jnp.bfloat16
MXU_DTYPE = BF16
HIGHEST = lax.Precision.HIGHEST
VMEM_LIMIT = 56 * 1024 * 1024


def _cparams(sem):
    return pltpu.CompilerParams(dimension_semantics=sem, vmem_limit_bytes=VMEM_LIMIT)


def _rms(x, g):
    return x * lax.rsqrt(jnp.mean(x * x, axis=-1, keepdims=True) + EPS) * g


def _dot(a, b):
    return jnp.dot(a, b, preferred_element_type=F32)


def _dot_nt(a, b, precision=None):
    return lax.dot_general(a, b, (((1,), (1,)), ((), ())), precision=precision,
                           preferred_element_type=F32)


def _sigmoid(x):
    return 1.0 / (1.0 + jnp.exp(-x))


def _norm_matmul_kernel(x_ref, g_ref, w_ref, o_ref, h_ref):
    @pl.when(pl.program_id(1) == 0)
    def _():
        h_ref[...] = _rms(x_ref[...], g_ref[...]).astype(h_ref.dtype)

    o_ref[...] = _dot(h_ref[...], w_ref[...]).astype(o_ref.dtype)


def norm_matmul(x, g, w, *, tm=1024, tn=1024):
    t, d = x.shape
    n = w.shape[1]
    tm = min(tm, t)
    assert t % tm == 0 and n % tn == 0
    return pl.pallas_call(
        _norm_matmul_kernel,
        grid=(t // tm, n // tn),
        in_specs=[pl.BlockSpec((tm, d), lambda i, j: (i, 0)),
                  pl.BlockSpec((1, d), lambda i, j: (0, 0)),
                  pl.BlockSpec((d, tn), lambda i, j: (0, j))],
        out_specs=pl.BlockSpec((tm, tn), lambda i, j: (i, j)),
        out_shape=jax.ShapeDtypeStruct((t, n), F32),
        scratch_shapes=[pltpu.VMEM((tm, d), MXU_DTYPE)],
        compiler_params=_cparams(("arbitrary", "arbitrary")),
        name="norm_matmul",
    )(x, g.reshape(1, d), w)


def _proj_residual_kernel(*refs, n_pairs):
    x_ref = refs[0]
    o_ref = refs[1 + 2 * n_pairs]
    acc = x_ref[...]
    for p in range(n_pairs):
        acc = acc + _dot(refs[1 + 2 * p][...], refs[2 + 2 * p][...])
    o_ref[...] = acc


def proj_residual(x, pairs, *, tm=1024, tn=1024):
    t, d = x.shape
    tm = min(tm, t)
    in_specs = [pl.BlockSpec((tm, tn), lambda i, j: (i, j))]
    args = [x]
    for a, w in pairs:
        k = a.shape[1]
        in_specs.append(pl.BlockSpec((tm, k), lambda i, j: (i, 0)))
        in_specs.append(pl.BlockSpec((k, tn), lambda i, j: (0, j)))
        args += [a, w]
    return pl.pallas_call(
        functools.partial(_proj_residual_kernel, n_pairs=len(pairs)),
        grid=(t // tm, d // tn),
        in_specs=in_specs,
        out_specs=pl.BlockSpec((tm, tn), lambda i, j: (i, j)),
        out_shape=jax.ShapeDtypeStruct((t, d), F32),
        compiler_params=_cparams(("arbitrary", "arbitrary")),
        name="proj_residual",
    )(*args)


CONV_HALO = 32
CONV_ROWS = 32


def _conv_kernel(av_ref, ag_ref, hv_ref, hg_ref, w_ref, b_ref, lg_ref, lb_ref, o_ref, s_ref, *, tq):
    i = pl.program_id(0)
    halo = hv_ref[...] * _sigmoid(hg_ref[...])
    s_ref[0:CONV_HALO, :] = jnp.where(i > 0, halo, 0.0)
    s_ref[CONV_HALO:, :] = av_ref[...] * _sigmoid(ag_ref[...])
    first = CONV_HALO - (CONV_WIDTH - 1)
    for c in range(tq // CONV_ROWS):
        base = c * CONV_ROWS + first
        acc = jnp.broadcast_to(b_ref[...], (CONV_ROWS, CONV_CH))
        for k in range(CONV_WIDTH):
            acc = acc + s_ref[base + k:base + k + CONV_ROWS, :] * w_ref[k:k + 1, :]
        mu = jnp.mean(acc, axis=-1, keepdims=True)
        cen = acc - mu
        var = jnp.mean(cen * cen, axis=-1, keepdims=True)
        y = cen * lax.rsqrt(var + EPS) * lg_ref[...] + lb_ref[...]
        o_ref[c * CONV_ROWS:(c + 1) * CONV_ROWS, :] = (y * _sigmoid(y)).astype(o_ref.dtype)


def conv_module(u, conv_w, conv_b, ln_g, ln_b, *, tq=512):
    t = u.shape[0]
    tq = min(tq, t)
    c = CONV_CH
    hb = tq // CONV_HALO
    w = jnp.pad(conv_w.reshape(CONV_WIDTH, c), ((0, 1), (0, 0)))
    row = lambda v: v.reshape(1, c)
    const = lambda i: (0, 0)
    return pl.pallas_call(
        functools.partial(_conv_kernel, tq=tq),
        grid=(t // tq,),
        in_specs=[pl.BlockSpec((tq, c), lambda i: (i, 0)),
                  pl.BlockSpec((tq, c), lambda i: (i, 1)),
                  pl.BlockSpec((CONV_HALO, c), lambda i: (jnp.maximum(i * hb - 1, 0), 0)),
                  pl.BlockSpec((CONV_HALO, c), lambda i: (jnp.maximum(i * hb - 1, 0), 1)),
                  pl.BlockSpec((CONV_WIDTH + 1, c), const),
                  pl.BlockSpec((1, c), const), pl.BlockSpec((1, c), const), pl.BlockSpec((1, c), const)],
        out_specs=pl.BlockSpec((tq, c), lambda i: (i, 0)),
        out_shape=jax.ShapeDtypeStruct((t, c), MXU_DTYPE),
        scratch_shapes=[pltpu.VMEM((tq + CONV_HALO, c), F32)],
        compiler_params=_cparams(("arbitrary",)),
        name="conv_module",
    )(u, u, u, u, w, row(conv_b), row(ln_g), row(ln_b))


def _moba_kernel(slopes_ref, q_ref, k_ref, v_ref, qg_ref, kg_ref, o_ref,
                 kn_s, v_s, km_s, m_s, l_s, acc_s, *, tq, t_total):
    h = pl.program_id(0)
    qi = pl.program_id(1)
    nb = t_total // MOBA_BLOCK
    blocks_per_tile = tq // MOBA_BLOCK

    @pl.when(qi == 0)
    def _():
        kn = _rms(k_ref[...], kg_ref[...])
        kn_s[...] = kn.astype(kn_s.dtype)
        km_s[...] = jnp.zeros_like(km_s)
        km_s[0:nb, :] = jnp.mean(kn.reshape(nb, MOBA_BLOCK, HEAD_DIM), axis=1)
        v_s[...] = v_ref[...].astype(v_s.dtype)

    qn = _rms(q_ref[...], qg_ref[...])
    gate = _dot_nt(qn, km_s[...], precision=HIGHEST)
    rowpos = qi * tq + lax.broadcasted_iota(jnp.int32, (tq, 1), 0)
    blk = lax.broadcasted_iota(jnp.int32, (tq, LANES), 1)
    blk_f = blk.astype(F32)
    qb = rowpos // MOBA_BLOCK
    past = blk < qb
    gate = jnp.where(past, gate, -jnp.inf)
    sel = blk == qb
    for _ in range(MOBA_TOPK):
        top = jnp.max(gate, axis=-1, keepdims=True)
        first = jnp.min(jnp.where(gate == top, blk_f, float(LANES)), axis=-1, keepdims=True)
        pick = blk_f == first
        sel = sel | (pick & past)
        gate = jnp.where(pick, -jnp.inf, gate)
    sel_f = jnp.where(sel, 1.0, 0.0)

    qs = (qn * HEAD_DIM ** -0.5).astype(MXU_DTYPE)
    slope = slopes_ref[h]
    m_s[...] = jnp.full_like(m_s, NEG)
    l_s[...] = jnp.zeros_like(l_s)
    acc_s[...] = jnp.zeros_like(acc_s)
    kcol = lax.broadcasted_iota(jnp.int32, (1, MOBA_BLOCK), 1)

    def step(j, carry):
        start = pl.multiple_of(j * MOBA_BLOCK, MOBA_BLOCK)
        kj = kn_s[pl.ds(start, MOBA_BLOCK), :]
        vj = v_s[pl.ds(start, MOBA_BLOCK), :]
        s = _dot_nt(qs, kj)
        chosen = jnp.max(jnp.where(blk == j, sel_f, 0.0), axis=-1, keepdims=True) > 0.0
        kpos = start + kcol
        bias = slope * (kpos - qi * tq).astype(F32)
        s = jnp.where(chosen & (kpos <= rowpos), s + bias, NEG)
        m_old = m_s[...]
        m_new = jnp.maximum(m_old, jnp.max(s, axis=-1, keepdims=True))
        alpha = jnp.exp(m_old - m_new)
        p = jnp.exp(s - m_new)
        l_s[...] = alpha * l_s[...] + jnp.sum(p, axis=-1, keepdims=True)
        acc_s[...] = alpha * acc_s[...] + _dot(p.astype(MXU_DTYPE), vj)
        m_s[...] = m_new
        return carry

    lax.fori_loop(0, (qi + 1) * blocks_per_tile, step, 0)
    o_ref[...] = (acc_s[...] / l_s[...]).astype(o_ref.dtype)


def moba_attention(u, q_norm, k_norm, *, col0, tq=1024):
    t = u.shape[0]
    tq = min(tq, t)
    nh = MOBA_HEADS
    slopes = jnp.asarray(2.0 ** (-8.0 * np.arange(1, nh + 1) / nh), dtype=F32)
    grid_spec = pltpu.PrefetchScalarGridSpec(
        num_scalar_prefetch=1,
        grid=(nh, t // tq),
        in_specs=[pl.BlockSpec((tq, HEAD_DIM), lambda h, i, s: (i, col0 + h)),
                  pl.BlockSpec((t, HEAD_DIM), lambda h, i, s: (0, col0 + nh + h)),
                  pl.BlockSpec((t, HEAD_DIM), lambda h, i, s: (0, col0 + 2 * nh + h)),
                  pl.BlockSpec((1, HEAD_DIM), lambda h, i, s: (0, 0)),
                  pl.BlockSpec((1, HEAD_DIM), lambda h, i, s: (0, 0))],
        out_specs=pl.BlockSpec((tq, HEAD_DIM), lambda h, i, s: (i, h)),
        scratch_shapes=[pltpu.VMEM((t, HEAD_DIM), MXU_DTYPE),
                        pltpu.VMEM((t, HEAD_DIM), MXU_DTYPE),
                        pltpu.VMEM((LANES, HEAD_DIM), F32),
                        pltpu.VMEM((tq, 1), F32),
                        pltpu.VMEM((tq, 1), F32),
                        pltpu.VMEM((tq, HEAD_DIM), F32)])
    return pl.pallas_call(
        functools.partial(_moba_kernel, tq=tq, t_total=t),
        grid_spec=grid_spec,
        out_shape=jax.ShapeDtypeStruct((t, MOBA_DIM), MXU_DTYPE),
        compiler_params=_cparams(("arbitrary", "arbitrary")),
        name="moba_attention",
    )(slopes, u, u, u, q_norm.reshape(1, HEAD_DIM), k_norm.reshape(1, HEAD_DIM))


def _swiglu_step(h, wg_ref, wu_ref, wd_ref):
    a = _dot(h, wg_ref[...])
    b = _dot(h, wu_ref[...])
    mid = (a * _sigmoid(a) * b).astype(MXU_DTYPE)
    return _dot(mid, wd_ref[...])


def _dense_ffn_kernel(x_ref, g_ref, wg_ref, wu_ref, wd_ref, o_ref, h_ref):
    @pl.when(pl.program_id(1) == 0)
    def _():
        x = x_ref[...]
        h_ref[...] = _rms(x, g_ref[...]).astype(h_ref.dtype)
        o_ref[...] = x

    o_ref[...] += _swiglu_step(h_ref[...], wg_ref, wu_ref, wd_ref)


def dense_ffn(x, g, wg, wu, wd, *, tm=512, tf=512):
    t, d = x.shape
    f = wg.shape[1]
    tm = min(tm, t)
    return pl.pallas_call(
        _dense_ffn_kernel,
        grid=(t // tm, f // tf),
        in_specs=[pl.BlockSpec((tm, d), lambda i, j: (i, 0)),
                  pl.BlockSpec((1, d), lambda i, j: (0, 0)),
                  pl.BlockSpec((d, tf), lambda i, j: (0, j)),
                  pl.BlockSpec((d, tf), lambda i, j: (0, j)),
                  pl.BlockSpec((tf, d), lambda i, j: (j, 0))],
        out_specs=pl.BlockSpec((tm, d), lambda i, j: (i, 0)),
        out_shape=jax.ShapeDtypeStruct((t, d), F32),
        scratch_shapes=[pltpu.VMEM((tm, d), MXU_DTYPE)],
        compiler_params=_cparams(("arbitrary", "arbitrary")),
        name="dense_ffn",
    )(x, g.reshape(1, d), wg, wu, wd)


def _grouped_ffn_kernel(te_ref, nu_ref, x_ref, wg_ref, wu_ref, wd_ref, o_ref, h_ref):
    i = pl.program_id(0)
    j = pl.program_id(1)

    @pl.when(j == 0)
    def _():
        h_ref[...] = x_ref[...].astype(h_ref.dtype)
        o_ref[...] = jnp.zeros_like(o_ref)

    @pl.when(i < nu_ref[0])
    def _():
        o_ref[...] += _swiglu_step(h_ref[...], wg_ref, wu_ref, wd_ref)


def grouped_ffn(xs, tile_expert, n_used, wg, wu, wd, *, tm, tf=512):
    m, d = xs.shape
    f = wg.shape[2]
    nf = f // tf

    def fcol(i, j, nu):
        return jnp.where(i < nu[0], j, nf - 1)

    grid_spec = pltpu.PrefetchScalarGridSpec(
        num_scalar_prefetch=2,
        grid=(m // tm, nf),
        in_specs=[pl.BlockSpec((tm, d), lambda i, j, te, nu: (i, 0)),
                  pl.BlockSpec((None, d, tf), lambda i, j, te, nu: (te[i], 0, fcol(i, j, nu))),
                  pl.BlockSpec((None, d, tf), lambda i, j, te, nu: (te[i], 0, fcol(i, j, nu))),
                  pl.BlockSpec((None, tf, d), lambda i, j, te, nu: (te[i], fcol(i, j, nu), 0))],
        out_specs=pl.BlockSpec((tm, d), lambda i, j, te, nu: (i, 0)),
        scratch_shapes=[pltpu.VMEM((tm, d), MXU_DTYPE)])
    return pl.pallas_call(
        _grouped_ffn_kernel,
        grid_spec=grid_spec,
        out_shape=jax.ShapeDtypeStruct((m, d), F32),
        compiler_params=_cparams(("arbitrary", "arbitrary")),
        name="grouped_ffn",
    )(tile_expert, n_used, xs, wg, wu, wd)


META_E, META_W, META_RANK = 0, 2, 4


def _router_kernel(x_ref, g_ref, wr_ref, h_ref, meta_ref, cnt_ref, carry_s, *, tm):
    @pl.when(pl.program_id(0) == 0)
    def _():
        carry_s[...] = jnp.zeros_like(carry_s)

    h = _rms(x_ref[...], g_ref[...])
    h_ref[...] = h
    logits = jnp.dot(h, wr_ref[...], precision=HIGHEST, preferred_element_type=F32)
    lane = lax.broadcasted_iota(jnp.int32, (tm, LANES), 1).astype(F32)
    lg = jnp.where(lane < N_EXPERTS, logits, -jnp.inf)

    def take_top(v):
        top = jnp.max(v, axis=-1, keepdims=True)
        idx = jnp.min(jnp.where(v == top, lane, float(LANES)), axis=-1, keepdims=True)
        return top, idx

    m1, i1 = take_top(lg)
    m2, i2 = take_top(jnp.where(lane == i1, -jnp.inf, lg))
    e = jnp.exp(m2 - m1)
    w1 = 1.0 / (1.0 + e)
    w2 = e / (1.0 + e)
    oh1 = lane == i1
    oh2 = lane == i2
    oh = jnp.where(oh1 | oh2, 1.0, 0.0)
    r = lax.broadcasted_iota(jnp.int32, (tm, tm), 0)
    c = lax.broadcasted_iota(jnp.int32, (tm, tm), 1)
    lower = jnp.where(c < r, 1.0, 0.0).astype(BF16)
    before = _dot(lower, oh.astype(BF16)) + carry_s[...]
    rank1 = jnp.sum(jnp.where(oh1, before, 0.0), axis=-1, keepdims=True)
    rank2 = jnp.sum(jnp.where(oh2, before, 0.0), axis=-1, keepdims=True)
    carry_s[...] += jnp.sum(oh, axis=0, keepdims=True)
    meta = jnp.zeros((tm, LANES), F32)
    for k, v in ((META_E, i1), (META_E + 1, i2), (META_W, w1), (META_W + 1, w2),
                 (META_RANK, rank1), (META_RANK + 1, rank2)):
        meta = jnp.where(lane == k, v, meta)
    meta_ref[...] = meta
    cnt_ref[...] = jnp.broadcast_to(carry_s[...], cnt_ref.shape)


def moe_router(x, g, w_router, *, tm=512):
    t, d = x.shape
    tm = min(tm, t)
    wr = jnp.pad(w_router, ((0, 0), (0, LANES - N_EXPERTS)))
    return pl.pallas_call(
        functools.partial(_router_kernel, tm=tm),
        grid=(t // tm,),
        in_specs=[pl.BlockSpec((tm, d), lambda i: (i, 0)),
                  pl.BlockSpec((1, d), lambda i: (0, 0)),
                  pl.BlockSpec((d, LANES), lambda i: (0, 0))],
        out_specs=[pl.BlockSpec((tm, d), lambda i: (i, 0)),
                   pl.BlockSpec((tm, LANES), lambda i: (i, 0)),
                   pl.BlockSpec((8, LANES), lambda i: (0, 0))],
        out_shape=[jax.ShapeDtypeStruct((t, d), F32),
                   jax.ShapeDtypeStruct((t, LANES), F32),
                   jax.ShapeDtypeStruct((8, LANES), F32)],
        scratch_shapes=[pltpu.VMEM((1, LANES), F32)],
        compiler_params=_cparams(("arbitrary",)),
        name="moe_router",
    )(x, g.reshape(1, d), wr)


DMA_WINDOW = 32


def _dispatch_kernel(p1_ref, p2_ref, h_ref, init_ref, xs_ref, sem, *, t):
    del init_ref

    def copies(tok):
        src = h_ref.at[pl.ds(tok, 1)]
        return (pltpu.make_async_copy(src, xs_ref.at[pl.ds(p1_ref[tok], 1)], sem.at[0]),
                pltpu.make_async_copy(src, xs_ref.at[pl.ds(p2_ref[tok], 1)], sem.at[1]))

    def wait_one():
        for c in copies(0):
            c.wait()

    def body(tok, carry):
        for c in copies(tok):
            c.start()

        @pl.when(tok >= DMA_WINDOW)
        def _():
            wait_one()
        return carry

    lax.fori_loop(0, t, body, 0)

    def drain(_, carry):
        wait_one()
        return carry

    lax.fori_loop(0, min(DMA_WINDOW, t), drain, 0)


def moe_dispatch(h, pos1, pos2, m_pad):
    t, d = h.shape
    smem = pl.BlockSpec(memory_space=pltpu.SMEM)
    anyspace = pl.BlockSpec(memory_space=pl.ANY)
    return pl.pallas_call(
        functools.partial(_dispatch_kernel, t=t),
        in_specs=[smem, smem, anyspace, anyspace],
        out_specs=anyspace,
        out_shape=jax.ShapeDtypeStruct((m_pad, d), h.dtype),
        scratch_shapes=[pltpu.SemaphoreType.DMA((2,))],
        input_output_aliases={3: 0},
        name="moe_dispatch",
    )(pos1, pos2, h, jnp.zeros((m_pad, d), h.dtype))


def _combine_kernel(p1_ref, p2_ref, x_ref, meta_ref, y_ref, o_ref, y1_s, y2_s, sem, *, tr):
    base = pl.program_id(0) * tr

    def copies(r, tok):
        return (pltpu.make_async_copy(y_ref.at[pl.ds(p1_ref[tok], 1)], y1_s.at[pl.ds(r, 1)], sem.at[0]),
                pltpu.make_async_copy(y_ref.at[pl.ds(p2_ref[tok], 1)], y2_s.at[pl.ds(r, 1)], sem.at[1]))

    def issue(r, carry):
        for c in copies(r, base + r):
            c.start()
        return carry

    lax.fori_loop(0, tr, issue, 0)

    def drain(r, carry):
        for c in copies(0, 0):
            c.wait()
        return carry

    lax.fori_loop(0, tr, drain, 0)
    meta = meta_ref[...]
    w1 = meta[:, META_W:META_W + 1]
    w2 = meta[:, META_W + 1:META_W + 2]
    o_ref[...] = x_ref[...] + w1 * y1_s[...] + w2 * y2_s[...]


def moe_combine(x, meta, y, pos1, pos2, *, tr=256):
    t, d = x.shape
    tr = min(tr, t)
    grid_spec = pltpu.PrefetchScalarGridSpec(
        num_scalar_prefetch=2,
        grid=(t // tr,),
        in_specs=[pl.BlockSpec((tr, d), lambda i, p1, p2: (i, 0)),
                  pl.BlockSpec((tr, LANES), lambda i, p1, p2: (i, 0)),
                  pl.BlockSpec(memory_space=pl.ANY)],
        out_specs=pl.BlockSpec((tr, d), lambda i, p1, p2: (i, 0)),
        scratch_shapes=[pltpu.VMEM((tr, d), F32), pltpu.VMEM((tr, d), F32),
                        pltpu.SemaphoreType.DMA((2,))])
    return pl.pallas_call(
        functools.partial(_combine_kernel, tr=tr),
        grid_spec=grid_spec,
        out_shape=jax.ShapeDtypeStruct((t, d), F32),
        compiler_params=_cparams(("arbitrary",)),
        name="moe_combine",
    )(pos1, pos2, x, meta, y)


def moe_ffn(x, ffn_norm, w_router, wg, wu, wd, *, tm=512):
    t, d = x.shape
    tm = min(tm, t)
    h, meta, cnt = moe_router(x, ffn_norm, w_router)
    counts = cnt[0, :N_EXPERTS].astype(jnp.int32)
    padded = (counts + tm - 1) // tm * tm
    ends = jnp.cumsum(padded)
    starts = ends - padded
    ids = meta[:, META_E:META_E + 2].astype(jnp.int32)
    ranks = meta[:, META_RANK:META_RANK + 2].astype(jnp.int32)
    pos = starts[ids] + ranks
    pos1, pos2 = pos[:, 0], pos[:, 1]
    m_pad = TOP_K * t + N_EXPERTS * tm
    n_tiles = m_pad // tm
    tile_start = jnp.arange(n_tiles, dtype=jnp.int32) * tm
    n_used = (ends[-1] // tm).astype(jnp.int32)
    tile_expert = jnp.sum((tile_start[:, None] >= ends[None, :]).astype(jnp.int32), axis=1)
    last_expert = jnp.sum((ends[-1] - 1 >= ends).astype(jnp.int32))
    tile_expert = jnp.where(tile_start < ends[-1], tile_expert, last_expert).astype(jnp.int32)
    xs = moe_dispatch(h, pos1, pos2, m_pad)
    y = grouped_ffn(xs, tile_expert, n_used.reshape(1), wg, wu, wd, tm=tm)
    return moe_combine(x, meta, y, pos1, pos2)


def _compress_kernel(c_ref, pe_ref, w1_ref, w2_ref, ng_ref, o_ref, *, nrows, half, normalize):
    c = c_ref[...].astype(MXU_DTYPE)
    top = _dot(c, w1_ref[0:half, :])
    bot = _dot(c, w1_ref[half:, :])
    bot_next = pltpu.roll(bot, nrows - 1, 0)
    pe_term = _dot(pe_ref[...].astype(MXU_DTYPE), w1_ref[...])
    hid = top + bot_next + pe_term
    out = _dot((hid * _sigmoid(hid)).astype(MXU_DTYPE), w2_ref[...])
    if normalize:
        out = _rms(out, ng_ref[...])
    valid = lax.broadcasted_iota(jnp.int32, (nrows, 1), 0) < nrows - 1
    o_ref[...] = jnp.where(valid, out, 0.0)


def compress(c, pe, w1, w2, norm_g, *, normalize):
    g, nrows, half = c.shape
    return pl.pallas_call(
        functools.partial(_compress_kernel, nrows=nrows, half=half, normalize=normalize),
        grid=(g,),
        in_specs=[pl.BlockSpec((None, nrows, half), lambda i: (i, 0, 0)),
                  pl.BlockSpec((1, 2 * half), lambda i: (0, 0)),
                  pl.BlockSpec((2 * half, CMP_HIDDEN), lambda i: (0, 0)),
                  pl.BlockSpec((CMP_HIDDEN, HEAD_DIM), lambda i: (0, 0)),
                  pl.BlockSpec((1, HEAD_DIM), lambda i: (0, 0))],
        out_specs=pl.BlockSpec((None, nrows, HEAD_DIM), lambda i: (i, 0, 0)),
        out_shape=jax.ShapeDtypeStruct((g, nrows, HEAD_DIM), F32),
        compiler_params=_cparams(("arbitrary",)),
        name="nsa_compress",
    )(c, pe.reshape(1, 2 * half), w1, w2, norm_g.reshape(1, HEAD_DIM))


def _nsa_cmp_kernel(slopes_ref, q_ref, kc_ref, vc_ref, qg_ref, ov_ref, oc_ref, sel_ref, *, tq, ncmp, nsel):
    g = pl.program_id(0)
    qi = pl.program_id(1)
    kc = kc_ref[...].astype(MXU_DTYPE)
    vc = vc_ref[...].astype(MXU_DTYPE)
    tpos = qi * tq + lax.broadcasted_iota(jnp.int32, (tq, 1), 0)
    cend = lax.broadcasted_iota(jnp.int32, (1, ncmp), 1) * CMP_STRIDE + (CMP_LEN - 1)
    dist = tpos - cend
    ok = dist >= 0
    dist_f = dist.astype(F32)
    any_ok = jnp.where(tpos >= CMP_LEN - 1, 1.0, 0.0)
    psum = jnp.zeros((tq, ncmp), F32)
    for r in range(NSA_GROUP):
        cols = slice(r * HEAD_DIM, (r + 1) * HEAD_DIM)
        qn = _rms(q_ref[:, cols], qg_ref[...])
        s = _dot_nt((qn * HEAD_DIM ** -0.5).astype(MXU_DTYPE), kc)
        s = jnp.where(ok, s - slopes_ref[g * NSA_GROUP + r] * dist_f, NEG)
        p = jnp.exp(s - jnp.max(s, axis=-1, keepdims=True))
        p = p * (any_ok / jnp.sum(p, axis=-1, keepdims=True))
        oc_ref[:, cols] = _dot(p.astype(MXU_DTYPE), vc).astype(oc_ref.dtype)
        psum = psum + p
    imp = _dot_nt(ov_ref[...], psum, precision=HIGHEST)
    blk = lax.broadcasted_iota(jnp.int32, (nsel, tq), 0)
    qb = (qi * tq + lax.broadcasted_iota(jnp.int32, (nsel, tq), 1)) // SLC_BLOCK
    forced = (blk == 0) | (blk == qb) | (blk == qb - 1)
    visible = blk <= qb
    imp = jnp.where(forced, jnp.inf, jnp.where(visible, imp, -jnp.inf))
    blk_f = blk.astype(F32)
    sel = jnp.zeros((nsel, tq), jnp.bool_)
    for _ in range(min(SLC_TOPN, nsel)):
        top = jnp.max(imp, axis=0, keepdims=True)
        first = jnp.min(jnp.where(imp == top, blk_f, float(nsel)), axis=0, keepdims=True)
        pick = blk_f == first
        sel = sel | pick
        imp = jnp.where(pick, -jnp.inf, imp)
    sel_f = jnp.where(sel & visible, 1.0, 0.0)
    sel_ref[...] = sel_f.T.astype(sel_ref.dtype)


def nsa_compressed(u, k_cmp, v_cmp, q_norm, *, tq=256):
    t = u.shape[0]
    tq = min(tq, t)
    ncmp = k_cmp.shape[1]
    nsel = t // SLC_BLOCK
    nstart = np.arange(ncmp)[:, None] * CMP_STRIDE
    lo = np.arange(nsel)[None, :] * SLC_BLOCK
    overlap = ((nstart < lo + SLC_BLOCK) & (nstart + CMP_LEN > lo) & (np.arange(ncmp)[:, None] < ncmp - 1))
    ov_t = jnp.asarray(overlap.T.astype(np.float32))
    slopes = jnp.asarray(2.0 ** (-8.0 * np.arange(1, NSA_HEADS + 1) / NSA_HEADS), dtype=F32)
    gw = NSA_GROUP * HEAD_DIM
    grid_spec = pltpu.PrefetchScalarGridSpec(
        num_scalar_prefetch=1,
        grid=(NSA_KV_HEADS, t // tq),
        in_specs=[pl.BlockSpec((tq, gw), lambda g, i, s: (i, g)),
                  pl.BlockSpec((None, ncmp, HEAD_DIM), lambda g, i, s: (g, 0, 0)),
                  pl.BlockSpec((None, ncmp, HEAD_DIM), lambda g, i, s: (g, 0, 0)),
                  pl.BlockSpec((1, HEAD_DIM), lambda g, i, s: (0, 0)),
                  pl.BlockSpec((nsel, ncmp), lambda g, i, s: (0, 0))],
        out_specs=[pl.BlockSpec((tq, gw), lambda g, i, s: (i, g)),
                   pl.BlockSpec((None, tq, nsel), lambda g, i, s: (g, i, 0))])
    return pl.pallas_call(
        functools.partial(_nsa_cmp_kernel, tq=tq, ncmp=ncmp, nsel=nsel),
        grid_spec=grid_spec,
        out_shape=[jax.ShapeDtypeStruct((t, NSA_DIM), F32),
                   jax.ShapeDtypeStruct((NSA_KV_HEADS, t, nsel), MXU_DTYPE)],
        compiler_params=_cparams(("arbitrary", "arbitrary")),
        name="nsa_compressed",
    )(slopes, u, k_cmp, v_cmp, q_norm.reshape(1, HEAD_DIM), ov_t)


def _flash_group_step(qs_all, kj, vj, allow, bias_rows, m_s, l_s, acc_s, *, tq):
    s_all = _dot_nt(qs_all, kj)
    for r in range(NSA_GROUP):
        rows = slice(r * tq, (r + 1) * tq)
        s = jnp.where(allow, s_all[rows, :] + bias_rows[r], NEG)
        m_old = m_s[rows, :]
        m_new = jnp.maximum(m_old, jnp.max(s, axis=-1, keepdims=True))
        alpha = jnp.exp(m_old - m_new)
        p = jnp.exp(s - m_new)
        l_s[rows, :] = alpha * l_s[rows, :] + jnp.sum(p, axis=-1, keepdims=True)
        acc_s[rows, :] = alpha * acc_s[rows, :] + _dot(p.astype(MXU_DTYPE), vj)
        m_s[rows, :] = m_new


def _group_queries(q_ref, qg_ref):
    parts = []
    for r in range(NSA_GROUP):
        qn = _rms(q_ref[:, r * HEAD_DIM:(r + 1) * HEAD_DIM], qg_ref[...])
        parts.append((qn * HEAD_DIM ** -0.5).astype(MXU_DTYPE))
    return jnp.concatenate(parts, axis=0)


def _init_flash(m_s, l_s, acc_s):
    m_s[...] = jnp.full_like(m_s, NEG)
    l_s[...] = jnp.zeros_like(l_s)
    acc_s[...] = jnp.zeros_like(acc_s)


def _nsa_slc_kernel(slopes_ref, q_ref, k_ref, v_ref, sel_ref, qg_ref, kg_ref, o_ref,
                    kn_s, v_s, m_s, l_s, acc_s, *, tq, nsel):
    g = pl.program_id(0)
    qi = pl.program_id(1)
    tk = tq
    blocks_per_tile = tk // SLC_BLOCK

    @pl.when(qi == 0)
    def _():
        kn_s[...] = _rms(k_ref[...], kg_ref[...]).astype(kn_s.dtype)
        v_s[...] = v_ref[...].astype(v_s.dtype)

    qs_all = _group_queries(q_ref, qg_ref)
    sel = sel_ref[...]
    _init_flash(m_s, l_s, acc_s)
    rowpos = qi * tq + lax.broadcasted_iota(jnp.int32, (tq, 1), 0)
    kcol = lax.broadcasted_iota(jnp.int32, (1, tk), 1)
    eb = lax.broadcasted_iota(jnp.int32, (nsel, tk), 0)
    ec = lax.broadcasted_iota(jnp.int32, (nsel, tk), 1) // SLC_BLOCK

    def step(j, carry):
        start = pl.multiple_of(j * tk, tk)
        kj = kn_s[pl.ds(start, tk), :]
        vj = v_s[pl.ds(start, tk), :]
        expand = jnp.where(eb == ec + j * blocks_per_tile, 1.0, 0.0).astype(MXU_DTYPE)
        chosen = _dot(sel, expand) > 0.5
        kpos = start + kcol
        allow = chosen & (kpos <= rowpos)
        rel = (kpos - qi * tq).astype(F32)
        bias_rows = [slopes_ref[g * NSA_GROUP + r] * rel for r in range(NSA_GROUP)]
        _flash_group_step(qs_all, kj, vj, allow, bias_rows, m_s, l_s, acc_s, tq=tq)
        return carry

    lax.fori_loop(0, qi + 1, step, 0)
    for r in range(NSA_GROUP):
        rows = slice(r * tq, (r + 1) * tq)
        o_ref[:, r * HEAD_DIM:(r + 1) * HEAD_DIM] = (acc_s[rows, :] / l_s[rows, :]).astype(o_ref.dtype)


def _nsa_win_kernel(slopes_ref, q_ref, k_ref, v_ref, oc_ref, os_ref, gl_ref, qg_ref, kg_ref, o_ref,
                    kn_s, v_s, m_s, l_s, acc_s, *, tq):
    g = pl.program_id(0)
    qi = pl.program_id(1)
    tk = tq

    @pl.when(qi == 0)
    def _():
        kn_s[...] = _rms(k_ref[...], kg_ref[...]).astype(kn_s.dtype)
        v_s[...] = v_ref[...].astype(v_s.dtype)

    qs_all = _group_queries(q_ref, qg_ref)
    _init_flash(m_s, l_s, acc_s)
    rowpos = qi * tq + lax.broadcasted_iota(jnp.int32, (tq, 1), 0)
    kcol = lax.broadcasted_iota(jnp.int32, (1, tk), 1)

    def step(j, carry):
        start = pl.multiple_of(j * tk, tk)
        kj = kn_s[pl.ds(start, tk), :]
        vj = v_s[pl.ds(start, tk), :]
        kpos = start + kcol
        dist = rowpos - kpos
        allow = (dist >= 0) & (dist < WINDOW)
        rel = (kpos - qi * tq).astype(F32)
        bias_rows = [slopes_ref[g * NSA_GROUP + r] * rel for r in range(NSA_GROUP)]
        _flash_group_step(qs_all, kj, vj, allow, bias_rows, m_s, l_s, acc_s, tq=tq)
        return carry

    first = jnp.maximum(qi - (WINDOW + tk - 1) // tk, 0)
    lax.fori_loop(first, qi + 1, step, 0)
    gates = _sigmoid(gl_ref[...])
    for r in range(NSA_GROUP):
        rows = slice(r * tq, (r + 1) * tq)
        cols = slice(r * HEAD_DIM, (r + 1) * HEAD_DIM)
        o_w = acc_s[rows, :] / l_s[rows, :]
        lane0 = 3 * r
        o = (gates[:, lane0:lane0 + 1] * oc_ref[:, cols] + gates[:, lane0 + 1:lane0 + 2] * os_ref[:, cols]
             + gates[:, lane0 + 2:lane0 + 3] * o_w)
        o_ref[:, cols] = o.astype(o_ref.dtype)


def _nsa_flash_scratch(t, tq):
    return [pltpu.VMEM((t, HEAD_DIM), MXU_DTYPE), pltpu.VMEM((t, HEAD_DIM), MXU_DTYPE),
            pltpu.VMEM((NSA_GROUP * tq, 1), F32), pltpu.VMEM((NSA_GROUP * tq, 1), F32),
            pltpu.VMEM((NSA_GROUP * tq, HEAD_DIM), F32)]


def _nsa_slopes():
    return jnp.asarray(2.0 ** (-8.0 * np.arange(1, NSA_HEADS + 1) / NSA_HEADS), dtype=F32)


def nsa_selected(u, sel, q_norm, k_norm, *, kcol, vcol, tq=256):
    t = u.shape[0]
    tq = min(tq, t)
    nsel = t // SLC_BLOCK
    gw = NSA_GROUP * HEAD_DIM
    grid_spec = pltpu.PrefetchScalarGridSpec(
        num_scalar_prefetch=1,
        grid=(NSA_KV_HEADS, t // tq),
        in_specs=[pl.BlockSpec((tq, gw), lambda g, i, s: (i, g)),
                  pl.BlockSpec((t, HEAD_DIM), lambda g, i, s: (0, kcol + g)),
                  pl.BlockSpec((t, HEAD_DIM), lambda g, i, s: (0, vcol + g)),
                  pl.BlockSpec((None, tq, nsel), lambda g, i, s: (g, i, 0)),
                  pl.BlockSpec((1, HEAD_DIM), lambda g, i, s: (0, 0)),
                  pl.BlockSpec((1, HEAD_DIM), lambda g, i, s: (0, 0))],
        out_specs=pl.BlockSpec((tq, gw), lambda g, i, s: (i, g)),
        scratch_shapes=_nsa_flash_scratch(t, tq))
    return pl.pallas_call(
        functools.partial(_nsa_slc_kernel, tq=tq, nsel=nsel),
        grid_spec=grid_spec,
        out_shape=jax.ShapeDtypeStruct((t, NSA_DIM), F32),
        compiler_params=_cparams(("arbitrary", "arbitrary")),
        name="nsa_selected",
    )(_nsa_slopes(), u, u, u, sel, q_norm.reshape(1, HEAD_DIM), k_norm.reshape(1, HEAD_DIM))


def nsa_window_merge(u, o_cmp, o_slc, gate_logits, q_norm, k_norm, *, kcol, vcol, tq=256):
    t = u.shape[0]
    tq = min(tq, t)
    gw = NSA_GROUP * HEAD_DIM
    gl = gate_logits.reshape(t, NSA_KV_HEADS, NSA_GROUP * 3)
    gl = jnp.pad(gl, ((0, 0), (0, 0), (0, LANES - NSA_GROUP * 3))).reshape(t, NSA_KV_HEADS * LANES)
    grid_spec = pltpu.PrefetchScalarGridSpec(
        num_scalar_prefetch=1,
        grid=(NSA_KV_HEADS, t // tq),
        in_specs=[pl.BlockSpec((tq, gw), lambda g, i, s: (i, g)),
                  pl.BlockSpec((t, HEAD_DIM), lambda g, i, s: (0, kcol + g)),
                  pl.BlockSpec((t, HEAD_DIM), lambda g, i, s: (0, vcol + g)),
                  pl.BlockSpec((tq, gw), lambda g, i, s: (i, g)),
                  pl.BlockSpec((tq, gw), lambda g, i, s: (i, g)),
                  pl.BlockSpec((tq, LANES), lambda g, i, s: (i, g)),
                  pl.BlockSpec((1, HEAD_DIM), lambda g, i, s: (0, 0)),
                  pl.BlockSpec((1, HEAD_DIM), lambda g, i, s: (0, 0))],
        out_specs=pl.BlockSpec((tq, gw), lambda g, i, s: (i, g)),
        scratch_shapes=_nsa_flash_scratch(t, tq))
    return pl.pallas_call(
        functools.partial(_nsa_win_kernel, tq=tq),
        grid_spec=grid_spec,
        out_shape=jax.ShapeDtypeStruct((t, NSA_DIM), MXU_DTYPE),
        compiler_params=_cparams(("arbitrary", "arbitrary")),
        name="nsa_window_merge",
    )(_nsa_slopes(), u, u, u, o_cmp, o_slc, gl, q_norm.reshape(1, HEAD_DIM), k_norm.reshape(1, HEAD_DIM))


def _mx(w):
    return w.astype(MXU_DTYPE)


def conv_moba_layer(x, attn_norm, w_in, conv_w, conv_b, conv_norm_g, conv_norm_b, q_norm, k_norm,
                    w_out, ffn_norm, w_gate, w_up, w_down):
    u = norm_matmul(x, attn_norm, _mx(w_in))
    a = conv_module(u, conv_w, conv_b, conv_norm_g, conv_norm_b)
    o = moba_attention(u, q_norm, k_norm, col0=2 * CONV_CH // HEAD_DIM)
    w_out = _mx(w_out)
    x = proj_residual(x, [(a, w_out[:CONV_CH]), (o, w_out[CONV_CH:])])
    return dense_ffn(x, ffn_norm, _mx(w_gate), _mx(w_up), _mx(w_down))


def nsa_moe_layer(x, attn_norm, w_in, q_norm, kc_norm, ks_norm, kw_norm, cmp_pe_k, cmp_w1_k, cmp_w2_k,
                  cmp_pe_v, cmp_w1_v, cmp_w2_v, w_out, ffn_norm, w_router, w_gate, w_up, w_down):
    t = x.shape[0]
    main = NSA_DIM + 6 * NSA_KV_DIM
    n_gate = 3 * NSA_HEADS
    w_in_p = jnp.pad(_mx(w_in), ((0, 0), (0, 1024 - n_gate)))
    u = norm_matmul(x, attn_norm, w_in_p)
    gate_logits = u[:, main:main + n_gate]
    nrows = t // CMP_STRIDE

    def blocks(col):
        v = u[:, col:col + NSA_KV_DIM].reshape(nrows, CMP_STRIDE, NSA_KV_HEADS, HEAD_DIM)
        return v.transpose(2, 0, 1, 3).reshape(NSA_KV_HEADS, nrows, CMP_STRIDE * HEAD_DIM)

    k_cmp = compress(blocks(NSA_DIM), cmp_pe_k, _mx(cmp_w1_k), _mx(cmp_w2_k), kc_norm, normalize=True)
    v_cmp = compress(blocks(NSA_DIM + NSA_KV_DIM), cmp_pe_v, _mx(cmp_w1_v), _mx(cmp_w2_v), kc_norm,
                     normalize=False)
    o_cmp, sel = nsa_compressed(u, k_cmp, v_cmp, q_norm)
    cb = lambda col: col // HEAD_DIM
    o_slc = nsa_selected(u, sel, q_norm, ks_norm, kcol=cb(NSA_DIM + 2 * NSA_KV_DIM),
                         vcol=cb(NSA_DIM + 3 * NSA_KV_DIM))
    o = nsa_window_merge(u, o_cmp, o_slc, gate_logits, q_norm, kw_norm,
                         kcol=cb(NSA_DIM + 4 * NSA_KV_DIM), vcol=cb(NSA_DIM + 5 * NSA_KV_DIM))
    x = proj_residual(x, [(o, _mx(w_out))])
    return moe_ffn(x, ffn_norm, w_router, _mx(w_gate), _mx(w_up), _mx(w_down))


def kernel(x, l0_attn_norm, l0_w_in, l0_conv_w, l0_conv_b, l0_conv_norm_g, l0_conv_norm_b, l0_q_norm, l0_k_norm, l0_w_out, l0_ffn_norm, l0_w_gate, l0_w_up, l0_w_down, l1_attn_norm, l1_w_in, l1_q_norm, l1_kc_norm, l1_ks_norm, l1_kw_norm, l1_cmp_pe_k, l1_cmp_w1_k, l1_cmp_w2_k, l1_cmp_pe_v, l1_cmp_w1_v, l1_cmp_w2_v, l1_w_out, l1_ffn_norm, l1_w_router, l1_w_gate, l1_w_up, l1_w_down, l2_attn_norm, l2_w_in, l2_conv_w, l2_conv_b, l2_conv_norm_g, l2_conv_norm_b, l2_q_norm, l2_k_norm, l2_w_out, l2_ffn_norm, l2_w_gate, l2_w_up, l2_w_down, l3_attn_norm, l3_w_in, l3_q_norm, l3_kc_norm, l3_ks_norm, l3_kw_norm, l3_cmp_pe_k, l3_cmp_w1_k, l3_cmp_w2_k, l3_cmp_pe_v, l3_cmp_w1_v, l3_cmp_w2_v, l3_w_out, l3_ffn_norm, l3_w_router, l3_w_gate, l3_w_up, l3_w_down):
    b, t, d = x.shape
    assert b == 1 and d == D_MODEL
    h = x.reshape(t, d)
    h = conv_moba_layer(h, l0_attn_norm, l0_w_in, l0_conv_w, l0_conv_b, l0_conv_norm_g, l0_conv_norm_b, l0_q_norm, l0_k_norm, l0_w_out, l0_ffn_norm, l0_w_gate, l0_w_up, l0_w_down)
    h = nsa_moe_layer(h, l1_attn_norm, l1_w_in, l1_q_norm, l1_kc_norm, l1_ks_norm, l1_kw_norm, l1_cmp_pe_k, l1_cmp_w1_k, l1_cmp_w2_k, l1_cmp_pe_v, l1_cmp_w1_v, l1_cmp_w2_v, l1_w_out, l1_ffn_norm, l1_w_router, l1_w_gate, l1_w_up, l1_w_down)
    h = conv_moba_layer(h, l2_attn_norm, l2_w_in, l2_conv_w, l2_conv_b, l2_conv_norm_g, l2_conv_norm_b, l2_q_norm, l2_k_norm, l2_w_out, l2_ffn_norm, l2_w_gate, l2_w_up, l2_w_down)
    h = nsa_moe_layer(h, l3_attn_norm, l3_w_in, l3_q_norm, l3_kc_norm, l3_ks_norm, l3_kw_norm, l3_cmp_pe_k, l3_cmp_w1_k, l3_cmp_w2_k, l3_cmp_pe_v, l3_cmp_w1_v, l3_cmp_w2_v, l3_w_out, l3_ffn_norm, l3_w_router, l3_w_gate, l3_w_up, l3_w_down)
    return h.reshape(b, t, d)
```

```python
import functools

import numpy as np
import jax
import jax.numpy as jnp
from jax import lax
from jax.experimental import pallas as pl
from jax.experimental.pallas import tpu as pltpu

D_MODEL = 2048
HEAD_DIM = 128
CONV_CH = D_MODEL // 2
CONV_WIDTH = 31
MOBA_HEADS = (D_MODEL // 2) // HEAD_DIM
MOBA_DIM = MOBA_HEADS * HEAD_DIM
MOBA_BLOCK = 256
MOBA_TOPK = 3
NSA_HEADS = D_MODEL // HEAD_DIM
NSA_KV_HEADS = NSA_HEADS // 4
NSA_GROUP = NSA_HEADS // NSA_KV_HEADS
NSA_DIM = NSA_HEADS * HEAD_DIM
NSA_KV_DIM = NSA_KV_HEADS * HEAD_DIM
CMP_LEN = 32
CMP_STRIDE = 16
CMP_HIDDEN = 256
SLC_BLOCK = 64
SLC_TOPN = 16
WINDOW = 512
D_FF = ((8 * D_MODEL) // 3 + 255) // 256 * 256
N_EXPERTS = 8
TOP_K = 2
EPS = 1e-6
NEG = -1e30

LANES = 128
F32 = jnp.float32
BF16 = jnp.bfloat16
MXU_DTYPE = BF16
HIGHEST = lax.Precision.HIGHEST
VMEM_LIMIT = 56 * 1024 * 1024


def _cparams(sem):
    return pltpu.CompilerParams(dimension_semantics=sem, vmem_limit_bytes=VMEM_LIMIT)


def _rms(x, g):
    return x * lax.rsqrt(jnp.mean(x * x, axis=-1, keepdims=True) + EPS) * g


def _dot(a, b):
    return jnp.dot(a, b, preferred_element_type=F32)


def _dot_nt(a, b, precision=None):
    return lax.dot_general(a, b, (((1,), (1,)), ((), ())), precision=precision,
                           preferred_element_type=F32)


def _sigmoid(x):
    return 1.0 / (1.0 + jnp.exp(-x))


def _norm_matmul_kernel(x_ref, g_ref, w_ref, o_ref, h_ref):
    @pl.when(pl.program_id(1) == 0)
    def _():
        h_ref[...] = _rms(x_ref[...], g_ref[...]).astype(h_ref.dtype)

    o_ref[...] = _dot(h_ref[...], w_ref[...]).astype(o_ref.dtype)


def norm_matmul(x, g, w, *, tm=1024, tn=1024):
    t, d = x.shape
    n = w.shape[1]
    tm = min(tm, t)
    assert t % tm == 0 and n % tn == 0
    return pl.pallas_call(
        _norm_matmul_kernel,
        grid=(t // tm, n // tn),
        in_specs=[pl.BlockSpec((tm, d), lambda i, j: (i, 0)),
                  pl.BlockSpec((1, d), lambda i, j: (0, 0)),
                  pl.BlockSpec((d, tn), lambda i, j: (0, j))],
        out_specs=pl.BlockSpec((tm, tn), lambda i, j: (i, j)),
        out_shape=jax.ShapeDtypeStruct((t, n), F32),
        scratch_shapes=[pltpu.VMEM((tm, d), MXU_DTYPE)],
        compiler_params=_cparams(("arbitrary", "arbitrary")),
        name="norm_matmul",
    )(x, g.reshape(1, d), w)


def _proj_residual_kernel(*refs, n_pairs):
    x_ref = refs[0]
    o_ref = refs[1 + 2 * n_pairs]
    acc = x_ref[...]
    for p in range(n_pairs):
        acc = acc + _dot(refs[1 + 2 * p][...], refs[2 + 2 * p][...])
    o_ref[...] = acc


def proj_residual(x, pairs, *, tm=1024, tn=1024):
    t, d = x.shape
    tm = min(tm, t)
    in_specs = [pl.BlockSpec((tm, tn), lambda i, j: (i, j))]
    args = [x]
    for a, w in pairs:
        k = a.shape[1]
        in_specs.append(pl.BlockSpec((tm, k), lambda i, j: (i, 0)))
        in_specs.append(pl.BlockSpec((k, tn), lambda i, j: (0, j)))
        args += [a, w]
    return pl.pallas_call(
        functools.partial(_proj_residual_kernel, n_pairs=len(pairs)),
        grid=(t // tm, d // tn),
        in_specs=in_specs,
        out_specs=pl.BlockSpec((tm, tn), lambda i, j: (i, j)),
        out_shape=jax.ShapeDtypeStruct((t, d), F32),
        compiler_params=_cparams(("arbitrary", "arbitrary")),
        name="proj_residual",
    )(*args)


CONV_HALO = 32
CONV_ROWS = 32


def _conv_kernel(av_ref, ag_ref, hv_ref, hg_ref, w_ref, b_ref, lg_ref, lb_ref, o_ref, s_ref, *, tq):
    i = pl.program_id(0)
    halo = hv_ref[...] * _sigmoid(hg_ref[...])
    s_ref[0:CONV_HALO, :] = jnp.where(i > 0, halo, 0.0)
    s_ref[CONV_HALO:, :] = av_ref[...] * _sigmoid(ag_ref[...])
    first = CONV_HALO - (CONV_WIDTH - 1)
    for c in range(tq // CONV_ROWS):
        base = c * CONV_ROWS + first
        acc = jnp.broadcast_to(b_ref[...], (CONV_ROWS, CONV_CH))
        for k in range(CONV_WIDTH):
            acc = acc + s_ref[base + k:base + k + CONV_ROWS, :] * w_ref[k:k + 1, :]
        mu = jnp.mean(acc, axis=-1, keepdims=True)
        cen = acc - mu
        var = jnp.mean(cen * cen, axis=-1, keepdims=True)
        y = cen * lax.rsqrt(var + EPS) * lg_ref[...] + lb_ref[...]
        o_ref[c * CONV_ROWS:(c + 1) * CONV_ROWS, :] = (y * _sigmoid(y)).astype(o_ref.dtype)


def conv_module(u, conv_w, conv_b, ln_g, ln_b, *, tq=512):
    t = u.shape[0]
    tq = min(tq, t)
    c = CONV_CH
    hb = tq // CONV_HALO
    w = jnp.pad(conv_w.reshape(CONV_WIDTH, c), ((0, 1), (0, 0)))
    row = lambda v: v.reshape(1, c)
    const = lambda i: (0, 0)
    return pl.pallas_call(
        functools.partial(_conv_kernel, tq=tq),
        grid=(t // tq,),
        in_specs=[pl.BlockSpec((tq, c), lambda i: (i, 0)),
                  pl.BlockSpec((tq, c), lambda i: (i, 1)),
                  pl.BlockSpec((CONV_HALO, c), lambda i: (jnp.maximum(i * hb - 1, 0), 0)),
                  pl.BlockSpec((CONV_HALO, c), lambda i: (jnp.maximum(i * hb - 1, 0), 1)),
                  pl.BlockSpec((CONV_WIDTH + 1, c), const),
                  pl.BlockSpec((1, c), const), pl.BlockSpec((1, c), const), pl.BlockSpec((1, c), const)],
        out_specs=pl.BlockSpec((tq, c), lambda i: (i, 0)),
        out_shape=jax.ShapeDtypeStruct((t, c), MXU_DTYPE),
        scratch_shapes=[pltpu.VMEM((tq + CONV_HALO, c), F32)],
        compiler_params=_cparams(("arbitrary",)),
        name="conv_module",
    )(u, u, u, u, w, row(conv_b), row(ln_g), row(ln_b))


ONES_ROWS = 16
LOG2E = 1.4426950408889634
QK_SCALE_LOG2 = HEAD_DIM ** -0.5 * LOG2E


def _kv_prologue(k_ref, v_ref, kg_ref, kaug_s, vt_s, *, t, tk, blk):
    kn = _rms(k_ref[...], kg_ref[...])
    kaug_s[:, 0:HEAD_DIM] = kn.astype(kaug_s.dtype)
    if blk is not None:
        lane_blk = lax.broadcasted_iota(jnp.int32, (t, LANES), 1)
        key_blk = lax.broadcasted_iota(jnp.int32, (t, LANES), 0) // blk
        kaug_s[:, HEAD_DIM:] = jnp.where(lane_blk == key_blk, 1.0, 0.0).astype(kaug_s.dtype)
    for c in range(t // tk):
        vt_s[c, 0:HEAD_DIM, :] = v_ref[c * tk:(c + 1) * tk, :].T.astype(vt_s.dtype)
        vt_s[c, HEAD_DIM:, :] = jnp.ones((ONES_ROWS, tk), vt_s.dtype)
    return kn


def _flash_init(slopes, sb_s, m_s, acc_s, *, tk, tq):
    key_off = lax.broadcasted_iota(jnp.int32, (tk, tq), 0).astype(F32)
    for r, slope in enumerate(slopes):
        sb_s[r] = (LOG2E * slope) * key_off
    m_s[...] = jnp.full_like(m_s, NEG)
    acc_s[...] = jnp.zeros_like(acc_s)


def _flash_step(j, *, t0, slopes, qaug_s, kaug_s, vt_s, sb_s, m_s, acc_s, tq, tk, mask):
    start = pl.multiple_of(j * tk, tk)
    kj = kaug_s[pl.ds(start, tk), :]
    vtj = vt_s[j]
    off = (start - t0).astype(F32)
    for r, slope in enumerate(slopes):
        cols = slice(r * tq, (r + 1) * tq)
        s_r = _dot_nt(kj, qaug_s[cols, :]) + sb_s[r]
        if mask is not None:
            s_r = jnp.where(mask, s_r, NEG)
        c = (LOG2E * slope) * off
        m_old = m_s[:, cols]
        m_new = jnp.maximum(m_old, jnp.max(s_r, axis=0, keepdims=True) + c)
        p_r = jnp.exp2(s_r - (m_new - c)).astype(MXU_DTYPE)
        acc_s[:, cols] = jnp.exp2(m_old - m_new) * acc_s[:, cols] + _dot(vtj, p_r)
        m_s[:, cols] = m_new


def _past_tiles(step, n):
    def body(j, carry):
        step(j, mask=None)
        return carry

    lax.fori_loop(0, n, body, 0)


def _flash_result(acc_s, r, tq):
    cols = slice(r * tq, (r + 1) * tq)
    o_t = acc_s[0:HEAD_DIM, cols] / acc_s[HEAD_DIM:HEAD_DIM + 1, cols]
    return o_t.T


def _flash_scratch(t, tq, tk, heads, kaug_cols):
    nq = heads * tq
    return [pltpu.VMEM((t, kaug_cols), MXU_DTYPE),
            pltpu.VMEM((t // tk, HEAD_DIM + ONES_ROWS, tk), MXU_DTYPE),
            pltpu.VMEM((nq, kaug_cols), MXU_DTYPE),
            pltpu.VMEM((heads, tk, tq), F32),
            pltpu.VMEM((1, nq), F32),
            pltpu.VMEM((HEAD_DIM + ONES_ROWS, nq), F32)]


def _causal_tile(tk, tq):
    return lax.broadcasted_iota(jnp.int32, (tk, tq), 0) <= lax.broadcasted_iota(jnp.int32, (tk, tq), 1)


def _moba_kernel(slopes_ref, q_ref, k_ref, v_ref, qg_ref, kg_ref, o_ref,
                 kaug_s, vt_s, qaug_s, sb_s, m_s, acc_s, km_s, *, tq, t_total):
    h = pl.program_id(0)
    qi = pl.program_id(1)
    nb = t_total // MOBA_BLOCK
    tk = tq

    @pl.when(qi == 0)
    def _():
        kn = _kv_prologue(k_ref, v_ref, kg_ref, kaug_s, vt_s, t=t_total, tk=tk, blk=MOBA_BLOCK)
        km_s[...] = jnp.zeros_like(km_s)
        km_s[0:nb, :] = jnp.mean(kn.reshape(nb, MOBA_BLOCK, HEAD_DIM), axis=1)

    qn = _rms(q_ref[...], qg_ref[...])
    gate = _dot_nt(qn, km_s[...], precision=HIGHEST)
    rowpos = qi * tq + lax.broadcasted_iota(jnp.int32, (tq, 1), 0)
    blk = lax.broadcasted_iota(jnp.int32, (tq, LANES), 1)
    blk_f = blk.astype(F32)
    qb = rowpos // MOBA_BLOCK
    past = blk < qb
    gate = jnp.where(past, gate, -jnp.inf)
    sel = blk == qb
    for _ in range(MOBA_TOPK):
        top = jnp.max(gate, axis=-1, keepdims=True)
        first = jnp.min(jnp.where(gate == top, blk_f, float(LANES)), axis=-1, keepdims=True)
        pick = blk_f == first
        sel = sel | (pick & past)
        gate = jnp.where(pick, -jnp.inf, gate)
    qaug_s[:, 0:HEAD_DIM] = (qn * QK_SCALE_LOG2).astype(qaug_s.dtype)
    qaug_s[:, HEAD_DIM:] = jnp.where(sel, 0.0, NEG).astype(qaug_s.dtype)

    slopes = [slopes_ref[h]]
    _flash_init(slopes, sb_s, m_s, acc_s, tk=tk, tq=tq)
    step = functools.partial(_flash_step, t0=qi * tq, slopes=slopes, qaug_s=qaug_s, kaug_s=kaug_s, vt_s=vt_s,
                             sb_s=sb_s, m_s=m_s, acc_s=acc_s, tq=tq, tk=tk)

    _past_tiles(step, qi)
    step(qi, mask=_causal_tile(tk, tq))
    o_ref[...] = _flash_result(acc_s, 0, tq).astype(o_ref.dtype)


def moba_attention(u, q_norm, k_norm, *, col0, tq=512):
    t = u.shape[0]
    tq = min(tq, t)
    assert t // MOBA_BLOCK <= LANES and tq % MOBA_BLOCK == 0
    nh = MOBA_HEADS
    slopes = jnp.asarray(2.0 ** (-8.0 * np.arange(1, nh + 1) / nh), dtype=F32)
    grid_spec = pltpu.PrefetchScalarGridSpec(
        num_scalar_prefetch=1,
        grid=(nh, t // tq),
        in_specs=[pl.BlockSpec((tq, HEAD_DIM), lambda h, i, s: (i, col0 + h)),
                  pl.BlockSpec((t, HEAD_DIM), lambda h, i, s: (0, col0 + nh + h)),
                  pl.BlockSpec((t, HEAD_DIM), lambda h, i, s: (0, col0 + 2 * nh + h)),
                  pl.BlockSpec((1, HEAD_DIM), lambda h, i, s: (0, 0)),
                  pl.BlockSpec((1, HEAD_DIM), lambda h, i, s: (0, 0))],
        out_specs=pl.BlockSpec((tq, HEAD_DIM), lambda h, i, s: (i, h)),
        scratch_shapes=_flash_scratch(t, tq, tq, 1, HEAD_DIM + LANES) + [pltpu.VMEM((LANES, HEAD_DIM), F32)])
    return pl.pallas_call(
        functools.partial(_moba_kernel, tq=tq, t_total=t),
        grid_spec=grid_spec,
        out_shape=jax.ShapeDtypeStruct((t, MOBA_DIM), MXU_DTYPE),
        compiler_params=_cparams(("arbitrary", "arbitrary")),
        name="moba_attention",
    )(slopes, u, u, u, q_norm.reshape(1, HEAD_DIM), k_norm.reshape(1, HEAD_DIM))


def _swiglu_step(h, wg_ref, wu_ref, wd_ref):
    a = _dot(h, wg_ref[...])
    b = _dot(h, wu_ref[...])
    mid = (a * _sigmoid(a) * b).astype(MXU_DTYPE)
    return _dot(mid, wd_ref[...])


def _dense_ffn_kernel(x_ref, g_ref, wg_ref, wu_ref, wd_ref, o_ref, h_ref):
    @pl.when(pl.program_id(1) == 0)
    def _():
        x = x_ref[...]
        h_ref[...] = _rms(x, g_ref[...]).astype(h_ref.dtype)
        o_ref[...] = x

    o_ref[...] += _swiglu_step(h_ref[...], wg_ref, wu_ref, wd_ref)


def dense_ffn(x, g, wg, wu, wd, *, tm=512, tf=512):
    t, d = x.shape
    f = wg.shape[1]
    tm = min(tm, t)
    return pl.pallas_call(
        _dense_ffn_kernel,
        grid=(t // tm, f // tf),
        in_specs=[pl.BlockSpec((tm, d), lambda i, j: (i, 0)),
                  pl.BlockSpec((1, d), lambda i, j: (0, 0)),
                  pl.BlockSpec((d, tf), lambda i, j: (0, j)),
                  pl.BlockSpec((d, tf), lambda i, j: (0, j)),
                  pl.BlockSpec((tf, d), lambda i, j: (j, 0))],
        out_specs=pl.BlockSpec((tm, d), lambda i, j: (i, 0)),
        out_shape=jax.ShapeDtypeStruct((t, d), F32),
        scratch_shapes=[pltpu.VMEM((tm, d), MXU_DTYPE)],
        compiler_params=_cparams(("arbitrary", "arbitrary")),
        name="dense_ffn",
    )(x, g.reshape(1, d), wg, wu, wd)


def _grouped_ffn_kernel(te_ref, nu_ref, x_ref, wg_ref, wu_ref, wd_ref, o_ref, h_ref):
    i = pl.program_id(0)
    j = pl.program_id(1)

    @pl.when(j == 0)
    def _():
        h_ref[...] = x_ref[...].astype(h_ref.dtype)
        o_ref[...] = jnp.zeros_like(o_ref)

    @pl.when(i < nu_ref[0])
    def _():
        o_ref[...] += _swiglu_step(h_ref[...], wg_ref, wu_ref, wd_ref)


def grouped_ffn(xs, tile_expert, n_used, wg, wu, wd, *, tm, tf=512):
    m, d = xs.shape
    f = wg.shape[2]
    nf = f // tf

    def fcol(i, j, nu):
        return jnp.where(i < nu[0], j, nf - 1)

    grid_spec = pltpu.PrefetchScalarGridSpec(
        num_scalar_prefetch=2,
        grid=(m // tm, nf),
        in_specs=[pl.BlockSpec((tm, d), lambda i, j, te, nu: (i, 0)),
                  pl.BlockSpec((None, d, tf), lambda i, j, te, nu: (te[i], 0, fcol(i, j, nu))),
                  pl.BlockSpec((None, d, tf), lambda i, j, te, nu: (te[i], 0, fcol(i, j, nu))),
                  pl.BlockSpec((None, tf, d), lambda i, j, te, nu: (te[i], fcol(i, j, nu), 0))],
        out_specs=pl.BlockSpec((tm, d), lambda i, j, te, nu: (i, 0)),
        scratch_shapes=[pltpu.VMEM((tm, d), MXU_DTYPE)])
    return pl.pallas_call(
        _grouped_ffn_kernel,
        grid_spec=grid_spec,
        out_shape=jax.ShapeDtypeStruct((m, d), F32),
        compiler_params=_cparams(("arbitrary", "arbitrary")),
        name="grouped_ffn",
    )(tile_expert, n_used, xs, wg, wu, wd)


META_E, META_W, META_RANK = 0, 2, 4


def _router_kernel(x_ref, g_ref, wr_ref, h_ref, meta_ref, cnt_ref, carry_s, *, tm):
    @pl.when(pl.program_id(0) == 0)
    def _():
        carry_s[...] = jnp.zeros_like(carry_s)

    h = _rms(x_ref[...], g_ref[...])
    h_ref[...] = h
    logits = jnp.dot(h, wr_ref[...], precision=HIGHEST, preferred_element_type=F32)
    lane = lax.broadcasted_iota(jnp.int32, (tm, LANES), 1).astype(F32)
    lg = jnp.where(lane < N_EXPERTS, logits, -jnp.inf)

    def take_top(v):
        top = jnp.max(v, axis=-1, keepdims=True)
        idx = jnp.min(jnp.where(v == top, lane, float(LANES)), axis=-1, keepdims=True)
        return top, idx

    m1, i1 = take_top(lg)
    m2, i2 = take_top(jnp.where(lane == i1, -jnp.inf, lg))
    e = jnp.exp(m2 - m1)
    w1 = 1.0 / (1.0 + e)
    w2 = e / (1.0 + e)
    oh1 = lane == i1
    oh2 = lane == i2
    oh = jnp.where(oh1 | oh2, 1.0, 0.0)
    r = lax.broadcasted_iota(jnp.int32, (tm, tm), 0)
    c = lax.broadcasted_iota(jnp.int32, (tm, tm), 1)
    lower = jnp.where(c < r, 1.0, 0.0).astype(BF16)
    before = _dot(lower, oh.astype(BF16)) + carry_s[...]
    rank1 = jnp.sum(jnp.where(oh1, before, 0.0), axis=-1, keepdims=True)
    rank2 = jnp.sum(jnp.where(oh2, before, 0.0), axis=-1, keepdims=True)
    carry_s[...] += jnp.sum(oh, axis=0, keepdims=True)
    meta = jnp.zeros((tm, LANES), F32)
    for k, v in ((META_E, i1), (META_E + 1, i2), (META_W, w1), (META_W + 1, w2),
                 (META_RANK, rank1), (META_RANK + 1, rank2)):
        meta = jnp.where(lane == k, v, meta)
    meta_ref[...] = meta
    cnt_ref[...] = jnp.broadcast_to(carry_s[...], cnt_ref.shape)


def moe_router(x, g, w_router, *, tm=512):
    t, d = x.shape
    tm = min(tm, t)
    wr = jnp.pad(w_router, ((0, 0), (0, LANES - N_EXPERTS)))
    return pl.pallas_call(
        functools.partial(_router_kernel, tm=tm),
        grid=(t // tm,),
        in_specs=[pl.BlockSpec((tm, d), lambda i: (i, 0)),
                  pl.BlockSpec((1, d), lambda i: (0, 0)),
                  pl.BlockSpec((d, LANES), lambda i: (0, 0))],
        out_specs=[pl.BlockSpec((tm, d), lambda i: (i, 0)),
                   pl.BlockSpec((tm, LANES), lambda i: (i, 0)),
                   pl.BlockSpec((8, LANES), lambda i: (0, 0))],
        out_shape=[jax.ShapeDtypeStruct((t, d), F32),
                   jax.ShapeDtypeStruct((t, LANES), F32),
                   jax.ShapeDtypeStruct((8, LANES), F32)],
        scratch_shapes=[pltpu.VMEM((1, LANES), F32)],
        compiler_params=_cparams(("arbitrary",)),
        name="moe_router",
    )(x, g.reshape(1, d), wr)


def _dispatch_kernel(p1_ref, p2_ref, h_ref, init_ref, xs_ref, sem, *, tr):
    del init_ref
    base = pl.program_id(0) * tr

    def copies(r, tok):
        src = h_ref.at[pl.ds(r, 1)]
        return (pltpu.make_async_copy(src, xs_ref.at[pl.ds(p1_ref[tok], 1)], sem.at[0]),
                pltpu.make_async_copy(src, xs_ref.at[pl.ds(p2_ref[tok], 1)], sem.at[1]))

    def issue(r, carry):
        for c in copies(r, base + r):
            c.start()
        return carry

    lax.fori_loop(0, tr, issue, 0)

    def drain(r, carry):
        for c in copies(0, 0):
            c.wait()
        return carry

    lax.fori_loop(0, tr, drain, 0)


def moe_dispatch(h, pos1, pos2, m_pad, *, tr=256):
    t, d = h.shape
    tr = min(tr, t)
    grid_spec = pltpu.PrefetchScalarGridSpec(
        num_scalar_prefetch=2,
        grid=(t // tr,),
        in_specs=[pl.BlockSpec((tr, d), lambda i, p1, p2: (i, 0)),
                  pl.BlockSpec(memory_space=pl.ANY)],
        out_specs=pl.BlockSpec(memory_space=pl.ANY),
        scratch_shapes=[pltpu.SemaphoreType.DMA((2,))])
    return pl.pallas_call(
        functools.partial(_dispatch_kernel, tr=tr),
        grid_spec=grid_spec,
        out_shape=jax.ShapeDtypeStruct((m_pad, d), h.dtype),
        input_output_aliases={3: 0},
        compiler_params=_cparams(("arbitrary",)),
        name="moe_dispatch",
    )(pos1, pos2, h, jnp.zeros((m_pad, d), h.dtype))


def _combine_kernel(p1_ref, p2_ref, x_ref, meta_ref, y_ref, o_ref, y1_s, y2_s, sem, *, tr):
    base = pl.program_id(0) * tr

    def copies(r, tok):
        return (pltpu.make_async_copy(y_ref.at[pl.ds(p1_ref[tok], 1)], y1_s.at[pl.ds(r, 1)], sem.at[0]),
                pltpu.make_async_copy(y_ref.at[pl.ds(p2_ref[tok], 1)], y2_s.at[pl.ds(r, 1)], sem.at[1]))

    def issue(r, carry):
        for c in copies(r, base + r):
            c.start()
        return carry

    lax.fori_loop(0, tr, issue, 0)

    def drain(r, carry):
        for c in copies(0, 0):
            c.wait()
        return carry

    lax.fori_loop(0, tr, drain, 0)
    meta = meta_ref[...]
    w1 = meta[:, META_W:META_W + 1]
    w2 = meta[:, META_W + 1:META_W + 2]
    o_ref[...] = x_ref[...] + w1 * y1_s[...] + w2 * y2_s[...]


def moe_combine(x, meta, y, pos1, pos2, *, tr=256):
    t, d = x.shape
    tr = min(tr, t)
    grid_spec = pltpu.PrefetchScalarGridSpec(
        num_scalar_prefetch=2,
        grid=(t // tr,),
        in_specs=[pl.BlockSpec((tr, d), lambda i, p1, p2: (i, 0)),
                  pl.BlockSpec((tr, LANES), lambda i, p1, p2: (i, 0)),
                  pl.BlockSpec(memory_space=pl.ANY)],
        out_specs=pl.BlockSpec((tr, d), lambda i, p1, p2: (i, 0)),
        scratch_shapes=[pltpu.VMEM((tr, d), F32), pltpu.VMEM((tr, d), F32),
                        pltpu.SemaphoreType.DMA((2,))])
    return pl.pallas_call(
        functools.partial(_combine_kernel, tr=tr),
        grid_spec=grid_spec,
        out_shape=jax.ShapeDtypeStruct((t, d), F32),
        compiler_params=_cparams(("arbitrary",)),
        name="moe_combine",
    )(pos1, pos2, x, meta, y)


def moe_ffn(x, ffn_norm, w_router, wg, wu, wd, *, tm=512):
    t, d = x.shape
    tm = min(tm, t)
    h, meta, cnt = moe_router(x, ffn_norm, w_router)
    counts = cnt[0, :N_EXPERTS].astype(jnp.int32)
    padded = (counts + tm - 1) // tm * tm
    ends = jnp.cumsum(padded)
    starts = ends - padded
    ids = meta[:, META_E:META_E + 2].astype(jnp.int32)
    ranks = meta[:, META_RANK:META_RANK + 2].astype(jnp.int32)
    pos = starts[ids] + ranks
    pos1, pos2 = pos[:, 0], pos[:, 1]
    m_pad = TOP_K * t + N_EXPERTS * tm
    n_tiles = m_pad // tm
    tile_start = jnp.arange(n_tiles, dtype=jnp.int32) * tm
    n_used = (ends[-1] // tm).astype(jnp.int32)
    tile_expert = jnp.sum((tile_start[:, None] >= ends[None, :]).astype(jnp.int32), axis=1)
    last_expert = jnp.sum((ends[-1] - 1 >= ends).astype(jnp.int32))
    tile_expert = jnp.where(tile_start < ends[-1], tile_expert, last_expert).astype(jnp.int32)
    xs = moe_dispatch(h, pos1, pos2, m_pad)
    y = grouped_ffn(xs, tile_expert, n_used.reshape(1), wg, wu, wd, tm=tm)
    return moe_combine(x, meta, y, pos1, pos2)


def _compress_kernel(c_ref, pe_ref, w1_ref, w2_ref, ng_ref, o_ref, *, nrows, half, normalize):
    c = c_ref[...].astype(MXU_DTYPE)
    top = _dot(c, w1_ref[0:half, :])
    bot = _dot(c, w1_ref[half:, :])
    bot_next = pltpu.roll(bot, nrows - 1, 0)
    pe_term = _dot(pe_ref[...].astype(MXU_DTYPE), w1_ref[...])
    hid = top + bot_next + pe_term
    out = _dot((hid * _sigmoid(hid)).astype(MXU_DTYPE), w2_ref[...])
    if normalize:
        out = _rms(out, ng_ref[...])
    valid = lax.broadcasted_iota(jnp.int32, (nrows, 1), 0) < nrows - 1
    o_ref[...] = jnp.where(valid, out, 0.0)


def compress(c, pe, w1, w2, norm_g, *, normalize):
    g, nrows, half = c.shape
    return pl.pallas_call(
        functools.partial(_compress_kernel, nrows=nrows, half=half, normalize=normalize),
        grid=(g,),
        in_specs=[pl.BlockSpec((None, nrows, half), lambda i: (i, 0, 0)),
                  pl.BlockSpec((1, 2 * half), lambda i: (0, 0)),
                  pl.BlockSpec((2 * half, CMP_HIDDEN), lambda i: (0, 0)),
                  pl.BlockSpec((CMP_HIDDEN, HEAD_DIM), lambda i: (0, 0)),
                  pl.BlockSpec((1, HEAD_DIM), lambda i: (0, 0))],
        out_specs=pl.BlockSpec((None, nrows, HEAD_DIM), lambda i: (i, 0, 0)),
        out_shape=jax.ShapeDtypeStruct((g, nrows, HEAD_DIM), F32),
        compiler_params=_cparams(("arbitrary",)),
        name="nsa_compress",
    )(c, pe.reshape(1, 2 * half), w1, w2, norm_g.reshape(1, HEAD_DIM))


def _nsa_cmp_kernel(slopes_ref, q_ref, kc_ref, vc_ref, qg_ref, ov_ref, oc_ref, sel_ref, *, tq, ncmp, nsel):
    g = pl.program_id(0)
    qi = pl.program_id(1)
    kc = kc_ref[...].astype(MXU_DTYPE)
    vc = vc_ref[...].astype(MXU_DTYPE)
    tpos = qi * tq + lax.broadcasted_iota(jnp.int32, (tq, 1), 0)
    cend = lax.broadcasted_iota(jnp.int32, (1, ncmp), 1) * CMP_STRIDE + (CMP_LEN - 1)
    dist = tpos - cend
    ok = dist >= 0
    dist_f = dist.astype(F32)
    any_ok = jnp.where(tpos >= CMP_LEN - 1, 1.0, 0.0)
    psum = jnp.zeros((tq, ncmp), F32)
    for r in range(NSA_GROUP):
        cols = slice(r * HEAD_DIM, (r + 1) * HEAD_DIM)
        qn = _rms(q_ref[:, cols], qg_ref[...])
        s = _dot_nt((qn * HEAD_DIM ** -0.5).astype(MXU_DTYPE), kc)
        s = jnp.where(ok, s - slopes_ref[g * NSA_GROUP + r] * dist_f, NEG)
        p = jnp.exp(s - jnp.max(s, axis=-1, keepdims=True))
        p = p * (any_ok / jnp.sum(p, axis=-1, keepdims=True))
        oc_ref[:, cols] = _dot(p.astype(MXU_DTYPE), vc).astype(oc_ref.dtype)
        psum = psum + p
    imp = _dot_nt(ov_ref[...], psum, precision=HIGHEST)
    blk = lax.broadcasted_iota(jnp.int32, (LANES, tq), 0)
    qb = (qi * tq + lax.broadcasted_iota(jnp.int32, (LANES, tq), 1)) // SLC_BLOCK
    forced = (blk == 0) | (blk == qb) | (blk == qb - 1)
    visible = blk <= qb
    imp = jnp.where(forced, jnp.inf, jnp.where(visible, imp, -jnp.inf))
    blk_f = blk.astype(F32)
    sel = jnp.zeros((LANES, tq), jnp.bool_)
    for _ in range(min(SLC_TOPN, nsel)):
        top = jnp.max(imp, axis=0, keepdims=True)
        first = jnp.min(jnp.where(imp == top, blk_f, float(LANES)), axis=0, keepdims=True)
        pick = blk_f == first
        sel = sel | pick
        imp = jnp.where(pick, -jnp.inf, imp)
    sel_f = jnp.where(sel & visible, 1.0, 0.0)
    sel_ref[...] = sel_f.T.astype(sel_ref.dtype)


def nsa_compressed(u, k_cmp, v_cmp, q_norm, *, tq=256):
    t = u.shape[0]
    tq = min(tq, t)
    ncmp = k_cmp.shape[1]
    nsel = t // SLC_BLOCK
    assert nsel <= LANES
    nstart = np.arange(ncmp)[:, None] * CMP_STRIDE
    lo = np.arange(LANES)[None, :] * SLC_BLOCK
    overlap = ((nstart < lo + SLC_BLOCK) & (nstart + CMP_LEN > lo) & (np.arange(ncmp)[:, None] < ncmp - 1))
    ov_t = jnp.asarray(overlap.T.astype(np.float32))
    gw = NSA_GROUP * HEAD_DIM
    grid_spec = pltpu.PrefetchScalarGridSpec(
        num_scalar_prefetch=1,
        grid=(NSA_KV_HEADS, t // tq),
        in_specs=[pl.BlockSpec((tq, gw), lambda g, i, s: (i, g)),
                  pl.BlockSpec((None, ncmp, HEAD_DIM), lambda g, i, s: (g, 0, 0)),
                  pl.BlockSpec((None, ncmp, HEAD_DIM), lambda g, i, s: (g, 0, 0)),
                  pl.BlockSpec((1, HEAD_DIM), lambda g, i, s: (0, 0)),
                  pl.BlockSpec((LANES, ncmp), lambda g, i, s: (0, 0))],
        out_specs=[pl.BlockSpec((tq, gw), lambda g, i, s: (i, g)),
                   pl.BlockSpec((None, tq, LANES), lambda g, i, s: (g, i, 0))])
    return pl.pallas_call(
        functools.partial(_nsa_cmp_kernel, tq=tq, ncmp=ncmp, nsel=nsel),
        grid_spec=grid_spec,
        out_shape=[jax.ShapeDtypeStruct((t, NSA_DIM), F32),
                   jax.ShapeDtypeStruct((NSA_KV_HEADS, t, LANES), MXU_DTYPE)],
        compiler_params=_cparams(("arbitrary", "arbitrary")),
        name="nsa_compressed",
    )(_nsa_slopes(), u, k_cmp, v_cmp, q_norm.reshape(1, HEAD_DIM), ov_t)


def _nsa_slopes():
    return jnp.asarray(2.0 ** (-8.0 * np.arange(1, NSA_HEADS + 1) / NSA_HEADS), dtype=F32)


def _group_queries(q_ref, qg_ref, qaug_s, tq):
    for r in range(NSA_GROUP):
        qn = _rms(q_ref[:, r * HEAD_DIM:(r + 1) * HEAD_DIM], qg_ref[...])
        qaug_s[r * tq:(r + 1) * tq, 0:HEAD_DIM] = (qn * QK_SCALE_LOG2).astype(qaug_s.dtype)


def _nsa_slc_kernel(slopes_ref, q_ref, k_ref, v_ref, sel_ref, qg_ref, kg_ref, o_ref,
                    kaug_s, vt_s, qaug_s, sb_s, m_s, acc_s, *, tq, t_total):
    g = pl.program_id(0)
    qi = pl.program_id(1)
    tk = tq

    @pl.when(qi == 0)
    def _():
        _kv_prologue(k_ref, v_ref, kg_ref, kaug_s, vt_s, t=t_total, tk=tk, blk=SLC_BLOCK)

    _group_queries(q_ref, qg_ref, qaug_s, tq)
    sel_bias = jnp.where(sel_ref[...].astype(F32) > 0.5, 0.0, NEG).astype(qaug_s.dtype)
    for r in range(NSA_GROUP):
        qaug_s[r * tq:(r + 1) * tq, HEAD_DIM:] = sel_bias
    slopes = [slopes_ref[g * NSA_GROUP + r] for r in range(NSA_GROUP)]
    _flash_init(slopes, sb_s, m_s, acc_s, tk=tk, tq=tq)
    step = functools.partial(_flash_step, t0=qi * tq, slopes=slopes, qaug_s=qaug_s, kaug_s=kaug_s, vt_s=vt_s,
                             sb_s=sb_s, m_s=m_s, acc_s=acc_s, tq=tq, tk=tk)

    _past_tiles(step, qi)
    step(qi, mask=_causal_tile(tk, tq))
    for r in range(NSA_GROUP):
        o_ref[:, r * HEAD_DIM:(r + 1) * HEAD_DIM] = _flash_result(acc_s, r, tq).astype(o_ref.dtype)


def _nsa_win_kernel(slopes_ref, q_ref, k_ref, v_ref, oc_ref, os_ref, gl_ref, qg_ref, kg_ref, o_ref,
                    kaug_s, vt_s, qaug_s, sb_s, m_s, acc_s, *, tq, t_total):
    g = pl.program_id(0)
    qi = pl.program_id(1)
    tk = tq
    nw = WINDOW // tk

    @pl.when(qi == 0)
    def _():
        _kv_prologue(k_ref, v_ref, kg_ref, kaug_s, vt_s, t=t_total, tk=tk, blk=None)

    _group_queries(q_ref, qg_ref, qaug_s, tq)
    slopes = [slopes_ref[g * NSA_GROUP + r] for r in range(NSA_GROUP)]
    _flash_init(slopes, sb_s, m_s, acc_s, tk=tk, tq=tq)
    step = functools.partial(_flash_step, t0=qi * tq, slopes=slopes, qaug_s=qaug_s, kaug_s=kaug_s, vt_s=vt_s,
                             sb_s=sb_s, m_s=m_s, acc_s=acc_s, tq=tq, tk=tk)
    key_i = lax.broadcasted_iota(jnp.int32, (tk, tq), 0)
    qry_i = lax.broadcasted_iota(jnp.int32, (tk, tq), 1)

    @pl.when(qi >= nw)
    def _():
        step(qi - nw, mask=key_i > qry_i)

    for back in range(nw - 1, 0, -1):
        @pl.when(qi >= back)
        def _():
            step(qi - back, mask=None)

    step(qi, mask=key_i <= qry_i)
    gates = _sigmoid(gl_ref[...])
    for r in range(NSA_GROUP):
        cols = slice(r * HEAD_DIM, (r + 1) * HEAD_DIM)
        lane0 = 3 * r
        o = (gates[:, lane0:lane0 + 1] * oc_ref[:, cols] + gates[:, lane0 + 1:lane0 + 2] * os_ref[:, cols]
             + gates[:, lane0 + 2:lane0 + 3] * _flash_result(acc_s, r, tq))
        o_ref[:, cols] = o.astype(o_ref.dtype)


def nsa_selected(u, sel, q_norm, k_norm, *, kcol, vcol, tq=256):
    t = u.shape[0]
    tq = min(tq, t)
    gw = NSA_GROUP * HEAD_DIM
    grid_spec = pltpu.PrefetchScalarGridSpec(
        num_scalar_prefetch=1,
        grid=(NSA_KV_HEADS, t // tq),
        in_specs=[pl.BlockSpec((tq, gw), lambda g, i, s: (i, g)),
                  pl.BlockSpec((t, HEAD_DIM), lambda g, i, s: (0, kcol + g)),
                  pl.BlockSpec((t, HEAD_DIM), lambda g, i, s: (0, vcol + g)),
                  pl.BlockSpec((None, tq, LANES), lambda g, i, s: (g, i, 0)),
                  pl.BlockSpec((1, HEAD_DIM), lambda g, i, s: (0, 0)),
                  pl.BlockSpec((1, HEAD_DIM), lambda g, i, s: (0, 0))],
        out_specs=pl.BlockSpec((tq, gw), lambda g, i, s: (i, g)),
        scratch_shapes=_flash_scratch(t, tq, tq, NSA_GROUP, HEAD_DIM + LANES))
    return pl.pallas_call(
        functools.partial(_nsa_slc_kernel, tq=tq, t_total=t),
        grid_spec=grid_spec,
        out_shape=jax.ShapeDtypeStruct((t, NSA_DIM), F32),
        compiler_params=_cparams(("arbitrary", "arbitrary")),
        name="nsa_selected",
    )(_nsa_slopes(), u, u, u, sel, q_norm.reshape(1, HEAD_DIM), k_norm.reshape(1, HEAD_DIM))


def nsa_window_merge(u, o_cmp, o_slc, gate_logits, q_norm, k_norm, *, kcol, vcol, tq=256):
    t = u.shape[0]
    tq = min(tq, t)
    assert WINDOW % tq == 0
    gw = NSA_GROUP * HEAD_DIM
    gl = gate_logits.reshape(t, NSA_KV_HEADS, NSA_GROUP * 3)
    gl = jnp.pad(gl, ((0, 0), (0, 0), (0, LANES - NSA_GROUP * 3))).reshape(t, NSA_KV_HEADS * LANES)
    grid_spec = pltpu.PrefetchScalarGridSpec(
        num_scalar_prefetch=1,
        grid=(NSA_KV_HEADS, t // tq),
        in_specs=[pl.BlockSpec((tq, gw), lambda g, i, s: (i, g)),
                  pl.BlockSpec((t, HEAD_DIM), lambda g, i, s: (0, kcol + g)),
                  pl.BlockSpec((t, HEAD_DIM), lambda g, i, s: (0, vcol + g)),
                  pl.BlockSpec((tq, gw), lambda g, i, s: (i, g)),
                  pl.BlockSpec((tq, gw), lambda g, i, s: (i, g)),
                  pl.BlockSpec((tq, LANES), lambda g, i, s: (i, g)),
                  pl.BlockSpec((1, HEAD_DIM), lambda g, i, s: (0, 0)),
                  pl.BlockSpec((1, HEAD_DIM), lambda g, i, s: (0, 0))],
        out_specs=pl.BlockSpec((tq, gw), lambda g, i, s: (i, g)),
        scratch_shapes=_flash_scratch(t, tq, tq, NSA_GROUP, HEAD_DIM))
    return pl.pallas_call(
        functools.partial(_nsa_win_kernel, tq=tq, t_total=t),
        grid_spec=grid_spec,
        out_shape=jax.ShapeDtypeStruct((t, NSA_DIM), MXU_DTYPE),
        compiler_params=_cparams(("arbitrary", "arbitrary")),
        name="nsa_window_merge",
    )(_nsa_slopes(), u, u, u, o_cmp, o_slc, gl, q_norm.reshape(1, HEAD_DIM), k_norm.reshape(1, HEAD_DIM))


def _mx(w):
    return w.astype(MXU_DTYPE)


def conv_moba_layer(x, attn_norm, w_in, conv_w, conv_b, conv_norm_g, conv_norm_b, q_norm, k_norm,
                    w_out, ffn_norm, w_gate, w_up, w_down):
    u = norm_matmul(x, attn_norm, _mx(w_in))
    a = conv_module(u, conv_w, conv_b, conv_norm_g, conv_norm_b)
    o = moba_attention(u, q_norm, k_norm, col0=2 * CONV_CH // HEAD_DIM)
    w_out = _mx(w_out)
    x = proj_residual(x, [(a, w_out[:CONV_CH]), (o, w_out[CONV_CH:])])
    return dense_ffn(x, ffn_norm, _mx(w_gate), _mx(w_up), _mx(w_down))


def nsa_moe_layer(x, attn_norm, w_in, q_norm, kc_norm, ks_norm, kw_norm, cmp_pe_k, cmp_w1_k, cmp_w2_k,
                  cmp_pe_v, cmp_w1_v, cmp_w2_v, w_out, ffn_norm, w_router, w_gate, w_up, w_down):
    t = x.shape[0]
    main = NSA_DIM + 6 * NSA_KV_DIM
    n_gate = 3 * NSA_HEADS
    w_in_p = jnp.pad(_mx(w_in), ((0, 0), (0, 1024 - n_gate)))
    u = norm_matmul(x, attn_norm, w_in_p)
    gate_logits = u[:, main:main + n_gate]
    nrows = t // CMP_STRIDE

    def blocks(col):
        v = u[:, col:col + NSA_KV_DIM].reshape(nrows, CMP_STRIDE, NSA_KV_HEADS, HEAD_DIM)
        return v.transpose(2, 0, 1, 3).reshape(NSA_KV_HEADS, nrows, CMP_STRIDE * HEAD_DIM)

    k_cmp = compress(blocks(NSA_DIM), cmp_pe_k, _mx(cmp_w1_k), _mx(cmp_w2_k), kc_norm, normalize=True)
    v_cmp = compress(blocks(NSA_DIM + NSA_KV_DIM), cmp_pe_v, _mx(cmp_w1_v), _mx(cmp_w2_v), kc_norm,
                     normalize=False)
    o_cmp, sel = nsa_compressed(u, k_cmp, v_cmp, q_norm)
    cb = lambda col: col // HEAD_DIM
    o_slc = nsa_selected(u, sel, q_norm, ks_norm, kcol=cb(NSA_DIM + 2 * NSA_KV_DIM),
                         vcol=cb(NSA_DIM + 3 * NSA_KV_DIM))
    o = nsa_window_merge(u, o_cmp, o_slc, gate_logits, q_norm, kw_norm,
                         kcol=cb(NSA_DIM + 4 * NSA_KV_DIM), vcol=cb(NSA_DIM + 5 * NSA_KV_DIM))
    x = proj_residual(x, [(o, _mx(w_out))])
    return moe_ffn(x, ffn_norm, w_router, _mx(w_gate), _mx(w_up), _mx(w_down))


def kernel(x, l0_attn_norm, l0_w_in, l0_conv_w, l0_conv_b, l0_conv_norm_g, l0_conv_norm_b, l0_q_norm, l0_k_norm, l0_w_out, l0_ffn_norm, l0_w_gate, l0_w_up, l0_w_down, l1_attn_norm, l1_w_in, l1_q_norm, l1_kc_norm, l1_ks_norm, l1_kw_norm, l1_cmp_pe_k, l1_cmp_w1_k, l1_cmp_w2_k, l1_cmp_pe_v, l1_cmp_w1_v, l1_cmp_w2_v, l1_w_out, l1_ffn_norm, l1_w_router, l1_w_gate, l1_w_up, l1_w_down, l2_attn_norm, l2_w_in, l2_conv_w, l2_conv_b, l2_conv_norm_g, l2_conv_norm_b, l2_q_norm, l2_k_norm, l2_w_out, l2_ffn_norm, l2_w_gate, l2_w_up, l2_w_down, l3_attn_norm, l3_w_in, l3_q_norm, l3_kc_norm, l3_ks_norm, l3_kw_norm, l3_cmp_pe_k, l3_cmp_w1_k, l3_cmp_w2_k, l3_cmp_pe_v, l3_cmp_w1_v, l3_cmp_w2_v, l3_w_out, l3_ffn_norm, l3_w_router, l3_w_gate, l3_w_up, l3_w_down):
    b, t, d = x.shape
    assert b == 1 and d == D_MODEL
    h = x.reshape(t, d)
    h = conv_moba_layer(h, l0_attn_norm, l0_w_in, l0_conv_w, l0_conv_b, l0_conv_norm_g, l0_conv_norm_b, l0_q_norm, l0_k_norm, l0_w_out, l0_ffn_norm, l0_w_gate, l0_w_up, l0_w_down)
    h = nsa_moe_layer(h, l1_attn_norm, l1_w_in, l1_q_norm, l1_kc_norm, l1_ks_norm, l1_kw_norm, l1_cmp_pe_k, l1_cmp_w1_k, l1_cmp_w2_k, l1_cmp_pe_v, l1_cmp_w1_v, l1_cmp_w2_v, l1_w_out, l1_ffn_norm, l1_w_router, l1_w_gate, l1_w_up, l1_w_down)
    h = conv_moba_layer(h, l2_attn_norm, l2_w_in, l2_conv_w, l2_conv_b, l2_conv_norm_g, l2_conv_norm_b, l2_q_norm, l2_k_norm, l2_w_out, l2_ffn_norm, l2_w_gate, l2_w_up, l2_w_down)
    h = nsa_moe_layer(h, l3_attn_norm, l3_w_in, l3_q_norm, l3_kc_norm, l3_ks_norm, l3_kw_norm, l3_cmp_pe_k, l3_cmp_w1_k, l3_cmp_w2_k, l3_cmp_pe_v, l3_cmp_w1_v, l3_cmp_w2_v, l3_w_out, l3_ffn_norm, l3_w_router, l3_w_gate, l3_w_up, l3_w_down)
    return h.reshape(b, t, d)
```

```python
import functools

import numpy as np
import jax
import jax.numpy as jnp
from jax import lax
from jax.experimental import pallas as pl
from jax.experimental.pallas import tpu as pltpu

D_MODEL = 2048
HEAD_DIM = 128
CONV_CH = D_MODEL // 2
CONV_WIDTH = 31
MOBA_HEADS = (D_MODEL // 2) // HEAD_DIM
MOBA_DIM = MOBA_HEADS * HEAD_DIM
MOBA_BLOCK = 256
MOBA_TOPK = 3
NSA_HEADS = D_MODEL // HEAD_DIM
NSA_KV_HEADS = NSA_HEADS // 4
NSA_GROUP = NSA_HEADS // NSA_KV_HEADS
NSA_DIM = NSA_HEADS * HEAD_DIM
NSA_KV_DIM = NSA_KV_HEADS * HEAD_DIM
CMP_LEN = 32
CMP_STRIDE = 16
CMP_HIDDEN = 256
SLC_BLOCK = 64
SLC_TOPN = 16
WINDOW = 512
D_FF = ((8 * D_MODEL) // 3 + 255) // 256 * 256
N_EXPERTS = 8
TOP_K = 2
EPS = 1e-6
NEG = -1e30

LANES = 128
F32 = jnp.float32
BF16 = jnp.bfloat16
MXU_DTYPE = BF16
HIGHEST = lax.Precision.HIGHEST
VMEM_LIMIT = 56 * 1024 * 1024


def _cparams(sem):
    return pltpu.CompilerParams(dimension_semantics=sem, vmem_limit_bytes=VMEM_LIMIT)


def _rms(x, g):
    return x * lax.rsqrt(jnp.mean(x * x, axis=-1, keepdims=True) + EPS) * g


def _dot(a, b):
    return jnp.dot(a, b, preferred_element_type=F32)


def _dot_nt(a, b, precision=None):
    return lax.dot_general(a, b, (((1,), (1,)), ((), ())), precision=precision,
                           preferred_element_type=F32)


def _sigmoid(x):
    return 1.0 / (1.0 + jnp.exp(-x))


def _norm_matmul_kernel(x_ref, g_ref, w_ref, o_ref, h_ref):
    @pl.when(pl.program_id(1) == 0)
    def _():
        h_ref[...] = _rms(x_ref[...], g_ref[...]).astype(h_ref.dtype)

    o_ref[...] = _dot(h_ref[...], w_ref[...]).astype(o_ref.dtype)


def norm_matmul(x, g, w, *, tm=1024, tn=1024):
    t, d = x.shape
    n = w.shape[1]
    tm = min(tm, t)
    assert t % tm == 0 and n % tn == 0
    return pl.pallas_call(
        _norm_matmul_kernel,
        grid=(t // tm, n // tn),
        in_specs=[pl.BlockSpec((tm, d), lambda i, j: (i, 0)),
                  pl.BlockSpec((1, d), lambda i, j: (0, 0)),
                  pl.BlockSpec((d, tn), lambda i, j: (0, j))],
        out_specs=pl.BlockSpec((tm, tn), lambda i, j: (i, j)),
        out_shape=jax.ShapeDtypeStruct((t, n), F32),
        scratch_shapes=[pltpu.VMEM((tm, d), MXU_DTYPE)],
        compiler_params=_cparams(("arbitrary", "arbitrary")),
        name="norm_matmul",
    )(x, g.reshape(1, d), w)


def _proj_residual_kernel(*refs, n_pairs):
    x_ref = refs[0]
    o_ref = refs[1 + 2 * n_pairs]
    acc = x_ref[...]
    for p in range(n_pairs):
        acc = acc + _dot(refs[1 + 2 * p][...], refs[2 + 2 * p][...])
    o_ref[...] = acc


def proj_residual(x, pairs, *, tm=1024, tn=1024):
    t, d = x.shape
    tm = min(tm, t)
    in_specs = [pl.BlockSpec((tm, tn), lambda i, j: (i, j))]
    args = [x]
    for a, w in pairs:
        k = a.shape[1]
        in_specs.append(pl.BlockSpec((tm, k), lambda i, j: (i, 0)))
        in_specs.append(pl.BlockSpec((k, tn), lambda i, j: (0, j)))
        args += [a, w]
    return pl.pallas_call(
        functools.partial(_proj_residual_kernel, n_pairs=len(pairs)),
        grid=(t // tm, d // tn),
        in_specs=in_specs,
        out_specs=pl.BlockSpec((tm, tn), lambda i, j: (i, j)),
        out_shape=jax.ShapeDtypeStruct((t, d), F32),
        compiler_params=_cparams(("arbitrary", "arbitrary")),
        name="proj_residual",
    )(*args)


CONV_HALO = 32
CONV_ROWS = 32


def _conv_kernel(av_ref, ag_ref, hv_ref, hg_ref, w_ref, b_ref, lg_ref, lb_ref, o_ref, s_ref, *, tq):
    i = pl.program_id(0)
    halo = hv_ref[...] * _sigmoid(hg_ref[...])
    s_ref[0:CONV_HALO, :] = jnp.where(i > 0, halo, 0.0)
    s_ref[CONV_HALO:, :] = av_ref[...] * _sigmoid(ag_ref[...])
    first = CONV_HALO - (CONV_WIDTH - 1)
    for c in range(tq // CONV_ROWS):
        base = c * CONV_ROWS + first
        acc = jnp.broadcast_to(b_ref[...], (CONV_ROWS, CONV_CH))
        for k in range(CONV_WIDTH):
            acc = acc + s_ref[base + k:base + k + CONV_ROWS, :] * w_ref[k:k + 1, :]
        mu = jnp.mean(acc, axis=-1, keepdims=True)
        cen = acc - mu
        var = jnp.mean(cen * cen, axis=-1, keepdims=True)
        y = cen * lax.rsqrt(var + EPS) * lg_ref[...] + lb_ref[...]
        o_ref[c * CONV_ROWS:(c + 1) * CONV_ROWS, :] = (y * _sigmoid(y)).astype(o_ref.dtype)


def conv_module(u, conv_w, conv_b, ln_g, ln_b, *, tq=512):
    t = u.shape[0]
    tq = min(tq, t)
    c = CONV_CH
    hb = tq // CONV_HALO
    w = jnp.pad(conv_w.reshape(CONV_WIDTH, c), ((0, 1), (0, 0)))
    row = lambda v: v.reshape(1, c)
    const = lambda i: (0, 0)
    return pl.pallas_call(
        functools.partial(_conv_kernel, tq=tq),
        grid=(t // tq,),
        in_specs=[pl.BlockSpec((tq, c), lambda i: (i, 0)),
                  pl.BlockSpec((tq, c), lambda i: (i, 1)),
                  pl.BlockSpec((CONV_HALO, c), lambda i: (jnp.maximum(i * hb - 1, 0), 0)),
                  pl.BlockSpec((CONV_HALO, c), lambda i: (jnp.maximum(i * hb - 1, 0), 1)),
                  pl.BlockSpec((CONV_WIDTH + 1, c), const),
                  pl.BlockSpec((1, c), const), pl.BlockSpec((1, c), const), pl.BlockSpec((1, c), const)],
        out_specs=pl.BlockSpec((tq, c), lambda i: (i, 0)),
        out_shape=jax.ShapeDtypeStruct((t, c), MXU_DTYPE),
        scratch_shapes=[pltpu.VMEM((tq + CONV_HALO, c), F32)],
        compiler_params=_cparams(("arbitrary",)),
        name="conv_module",
    )(u, u, u, u, w, row(conv_b), row(ln_g), row(ln_b))


ONES_ROWS = 16
LOG2E = 1.4426950408889634
QK_SCALE_LOG2 = HEAD_DIM ** -0.5 * LOG2E


def _kv_prologue(k_ref, v_ref, kg_ref, kaug_s, vt_s, *, t, tk, blk):
    kn = _rms(k_ref[...], kg_ref[...])
    kaug_s[:, 0:HEAD_DIM] = kn.astype(kaug_s.dtype)
    if blk is not None:
        lane_blk = lax.broadcasted_iota(jnp.int32, (t, LANES), 1)
        key_blk = lax.broadcasted_iota(jnp.int32, (t, LANES), 0) // blk
        kaug_s[:, HEAD_DIM:] = jnp.where(lane_blk == key_blk, 1.0, 0.0).astype(kaug_s.dtype)
    for c in range(t // tk):
        vt_s[c, 0:HEAD_DIM, :] = v_ref[c * tk:(c + 1) * tk, :].T.astype(vt_s.dtype)
        vt_s[c, HEAD_DIM:, :] = jnp.ones((ONES_ROWS, tk), vt_s.dtype)
    return kn


PLAIN, CAUSAL, WINDOW_TAIL = 0, 1, 2


def _tile_pattern(kind, tk, tq):
    key_i = lax.broadcasted_iota(jnp.int32, (tk, tq), 0)
    qry_i = lax.broadcasted_iota(jnp.int32, (tk, tq), 1)
    if kind == CAUSAL:
        return key_i <= qry_i
    if kind == WINDOW_TAIL:
        return key_i > qry_i
    return None


def _fill_bias(sb_s, slopes, kinds, *, tk, tq):
    key_off = lax.broadcasted_iota(jnp.int32, (tk, tq), 0).astype(F32)
    for r, slope in enumerate(slopes):
        bias = (LOG2E * slope) * key_off
        for v, kind in enumerate(kinds):
            pattern = _tile_pattern(kind, tk, tq)
            sb_s[v, r] = bias if pattern is None else jnp.where(pattern, bias, NEG)


def _flash_init(m_s, acc_s):
    m_s[...] = jnp.full_like(m_s, NEG)
    acc_s[...] = jnp.zeros_like(acc_s)


def _qk_tile(j, slot, *, heads, qaug_s, kaug_s, s_s, tq, tk):
    kj = kaug_s[pl.ds(pl.multiple_of(j * tk, tk), tk), :]
    for r in range(heads):
        s_s[slot, r] = _dot_nt(kj, qaug_s[r * tq:(r + 1) * tq, :])


def _softmax_pv_tile(j, slot, variant, *, t0, slopes, vt_s, s_s, sb_s, m_s, acc_s, tq, tk):
    vtj = vt_s[j]
    off = (j * tk - t0).astype(F32)
    for r, slope in enumerate(slopes):
        cols = slice(r * tq, (r + 1) * tq)
        s_r = s_s[slot, r] + sb_s[variant, r]
        c = (LOG2E * slope) * off
        m_old = m_s[:, cols]
        m_new = jnp.maximum(m_old, jnp.max(s_r, axis=0, keepdims=True) + c)
        p_r = jnp.exp2(s_r - (m_new - c)).astype(MXU_DTYPE)
        acc_s[:, cols] = jnp.exp2(m_old - m_new) * acc_s[:, cols] + _dot(vtj, p_r)
        m_s[:, cols] = m_new


def _causal_sweep(qk, spv, tile_at, n_past):
    qk(tile_at(0), 0)

    def pair(i, carry):
        k = 2 * i
        qk(tile_at(k + 1), 1)
        spv(tile_at(k), 0, PLAIN)
        qk(tile_at(k + 2), 0)
        spv(tile_at(k + 1), 1, PLAIN)
        return carry

    lax.fori_loop(0, n_past // 2, pair, 0)
    k = (n_past // 2) * 2

    @pl.when(n_past % 2 == 1)
    def _():
        qk(tile_at(k + 1), 1)
        spv(tile_at(k), 0, PLAIN)
        spv(tile_at(k + 1), 1, CAUSAL)

    @pl.when(n_past % 2 == 0)
    def _():
        spv(tile_at(k), 0, CAUSAL)


def _flash_result(acc_s, r, tq):
    cols = slice(r * tq, (r + 1) * tq)
    o_t = acc_s[0:HEAD_DIM, cols] / acc_s[HEAD_DIM:HEAD_DIM + 1, cols]
    return o_t.T


def _flash_scratch(t, tq, tk, heads, kaug_cols, n_patterns, n_slots):
    nq = heads * tq
    return [pltpu.VMEM((t, kaug_cols), MXU_DTYPE),
            pltpu.VMEM((t // tk, HEAD_DIM + ONES_ROWS, tk), MXU_DTYPE),
            pltpu.VMEM((nq, kaug_cols), MXU_DTYPE),
            pltpu.VMEM((n_patterns, heads, tk, tq), F32),
            pltpu.VMEM((n_slots, heads, tk, tq), F32),
            pltpu.VMEM((1, nq), F32),
            pltpu.VMEM((HEAD_DIM + ONES_ROWS, nq), F32)]


def _flash_stages(*, t0, slopes, qaug_s, kaug_s, vt_s, sb_s, s_s, m_s, acc_s, tq, tk):
    qk = functools.partial(_qk_tile, heads=len(slopes), qaug_s=qaug_s, kaug_s=kaug_s, s_s=s_s, tq=tq, tk=tk)
    spv = functools.partial(_softmax_pv_tile, t0=t0, slopes=slopes, vt_s=vt_s, s_s=s_s, sb_s=sb_s, m_s=m_s,
                            acc_s=acc_s, tq=tq, tk=tk)
    return qk, spv


def _moba_kernel(slopes_ref, q_ref, k_ref, v_ref, qg_ref, kg_ref, o_ref,
                 kaug_s, vt_s, qaug_s, sb_s, s_s, m_s, acc_s, km_s, *, tq, t_total):
    h = pl.program_id(0)
    qi = pl.program_id(1)
    nb = t_total // MOBA_BLOCK
    tk = tq

    slopes = [slopes_ref[h]]

    @pl.when(qi == 0)
    def _():
        kn = _kv_prologue(k_ref, v_ref, kg_ref, kaug_s, vt_s, t=t_total, tk=tk, blk=MOBA_BLOCK)
        km_s[...] = jnp.zeros_like(km_s)
        km_s[0:nb, :] = jnp.mean(kn.reshape(nb, MOBA_BLOCK, HEAD_DIM), axis=1)
        _fill_bias(sb_s, slopes, (PLAIN, CAUSAL), tk=tk, tq=tq)

    qn = _rms(q_ref[...], qg_ref[...])
    gate = _dot_nt(qn, km_s[...], precision=HIGHEST)
    rowpos = qi * tq + lax.broadcasted_iota(jnp.int32, (tq, 1), 0)
    blk = lax.broadcasted_iota(jnp.int32, (tq, LANES), 1)
    blk_f = blk.astype(F32)
    qb = rowpos // MOBA_BLOCK
    past = blk < qb
    gate = jnp.where(past, gate, -jnp.inf)
    sel = blk == qb
    for _ in range(MOBA_TOPK):
        top = jnp.max(gate, axis=-1, keepdims=True)
        first = jnp.min(jnp.where(gate == top, blk_f, float(LANES)), axis=-1, keepdims=True)
        pick = blk_f == first
        sel = sel | (pick & past)
        gate = jnp.where(pick, -jnp.inf, gate)
    qaug_s[:, 0:HEAD_DIM] = (qn * QK_SCALE_LOG2).astype(qaug_s.dtype)
    qaug_s[:, HEAD_DIM:] = jnp.where(sel, 0.0, NEG).astype(qaug_s.dtype)

    _flash_init(m_s, acc_s)
    qk, spv = _flash_stages(t0=qi * tq, slopes=slopes, qaug_s=qaug_s, kaug_s=kaug_s, vt_s=vt_s, sb_s=sb_s,
                            s_s=s_s, m_s=m_s, acc_s=acc_s, tq=tq, tk=tk)
    _causal_sweep(qk, spv, lambda k: k, qi)
    o_ref[...] = _flash_result(acc_s, 0, tq).astype(o_ref.dtype)


def moba_attention(u, q_norm, k_norm, *, col0, tq=512):
    t = u.shape[0]
    tq = min(tq, t)
    assert t // MOBA_BLOCK <= LANES and tq % MOBA_BLOCK == 0
    nh = MOBA_HEADS
    slopes = jnp.asarray(2.0 ** (-8.0 * np.arange(1, nh + 1) / nh), dtype=F32)
    grid_spec = pltpu.PrefetchScalarGridSpec(
        num_scalar_prefetch=1,
        grid=(nh, t // tq),
        in_specs=[pl.BlockSpec((tq, HEAD_DIM), lambda h, i, s: (i, col0 + h)),
                  pl.BlockSpec((t, HEAD_DIM), lambda h, i, s: (0, col0 + nh + h)),
                  pl.BlockSpec((t, HEAD_DIM), lambda h, i, s: (0, col0 + 2 * nh + h)),
                  pl.BlockSpec((1, HEAD_DIM), lambda h, i, s: (0, 0)),
                  pl.BlockSpec((1, HEAD_DIM), lambda h, i, s: (0, 0))],
        out_specs=pl.BlockSpec((tq, HEAD_DIM), lambda h, i, s: (i, h)),
        scratch_shapes=_flash_scratch(t, tq, tq, 1, HEAD_DIM + LANES, 2, 2) + [pltpu.VMEM((LANES, HEAD_DIM), F32)])
    return pl.pallas_call(
        functools.partial(_moba_kernel, tq=tq, t_total=t),
        grid_spec=grid_spec,
        out_shape=jax.ShapeDtypeStruct((t, MOBA_DIM), MXU_DTYPE),
        compiler_params=_cparams(("arbitrary", "arbitrary")),
        name="moba_attention",
    )(slopes, u, u, u, q_norm.reshape(1, HEAD_DIM), k_norm.reshape(1, HEAD_DIM))


def _swiglu_step(h, wg_ref, wu_ref, wd_ref):
    a = _dot(h, wg_ref[...])
    b = _dot(h, wu_ref[...])
    mid = (a * _sigmoid(a) * b).astype(MXU_DTYPE)
    return _dot(mid, wd_ref[...])


def _dense_ffn_kernel(x_ref, g_ref, wg_ref, wu_ref, wd_ref, o_ref, h_ref):
    @pl.when(pl.program_id(1) == 0)
    def _():
        x = x_ref[...]
        h_ref[...] = _rms(x, g_ref[...]).astype(h_ref.dtype)
        o_ref[...] = x

    o_ref[...] += _swiglu_step(h_ref[...], wg_ref, wu_ref, wd_ref)


def dense_ffn(x, g, wg, wu, wd, *, tm=512, tf=512):
    t, d = x.shape
    f = wg.shape[1]
    tm = min(tm, t)
    return pl.pallas_call(
        _dense_ffn_kernel,
        grid=(t // tm, f // tf),
        in_specs=[pl.BlockSpec((tm, d), lambda i, j: (i, 0)),
                  pl.BlockSpec((1, d), lambda i, j: (0, 0)),
                  pl.BlockSpec((d, tf), lambda i, j: (0, j)),
                  pl.BlockSpec((d, tf), lambda i, j: (0, j)),
                  pl.BlockSpec((tf, d), lambda i, j: (j, 0))],
        out_specs=pl.BlockSpec((tm, d), lambda i, j: (i, 0)),
        out_shape=jax.ShapeDtypeStruct((t, d), F32),
        scratch_shapes=[pltpu.VMEM((tm, d), MXU_DTYPE)],
        compiler_params=_cparams(("arbitrary", "arbitrary")),
        name="dense_ffn",
    )(x, g.reshape(1, d), wg, wu, wd)


def _grouped_ffn_kernel(te_ref, nu_ref, x_ref, wg_ref, wu_ref, wd_ref, o_ref, h_ref):
    i = pl.program_id(0)
    j = pl.program_id(1)

    @pl.when(j == 0)
    def _():
        h_ref[...] = x_ref[...].astype(h_ref.dtype)
        o_ref[...] = jnp.zeros_like(o_ref)

    @pl.when(i < nu_ref[0])
    def _():
        o_ref[...] += _swiglu_step(h_ref[...], wg_ref, wu_ref, wd_ref)


def grouped_ffn(xs, tile_expert, n_used, wg, wu, wd, *, tm, tf=512):
    m, d = xs.shape
    f = wg.shape[2]
    nf = f // tf

    def fcol(i, j, nu):
        return jnp.where(i < nu[0], j, nf - 1)

    grid_spec = pltpu.PrefetchScalarGridSpec(
        num_scalar_prefetch=2,
        grid=(m // tm, nf),
        in_specs=[pl.BlockSpec((tm, d), lambda i, j, te, nu: (i, 0)),
                  pl.BlockSpec((None, d, tf), lambda i, j, te, nu: (te[i], 0, fcol(i, j, nu))),
                  pl.BlockSpec((None, d, tf), lambda i, j, te, nu: (te[i], 0, fcol(i, j, nu))),
                  pl.BlockSpec((None, tf, d), lambda i, j, te, nu: (te[i], fcol(i, j, nu), 0))],
        out_specs=pl.BlockSpec((tm, d), lambda i, j, te, nu: (i, 0)),
        scratch_shapes=[pltpu.VMEM((tm, d), MXU_DTYPE)])
    return pl.pallas_call(
        _grouped_ffn_kernel,
        grid_spec=grid_spec,
        out_shape=jax.ShapeDtypeStruct((m, d), F32),
        compiler_params=_cparams(("arbitrary", "arbitrary")),
        name="grouped_ffn",
    )(tile_expert, n_used, xs, wg, wu, wd)


META_E, META_W, META_RANK = 0, 2, 4


def _router_kernel(x_ref, g_ref, wr_ref, h_ref, meta_ref, cnt_ref, carry_s, *, tm):
    @pl.when(pl.program_id(0) == 0)
    def _():
        carry_s[...] = jnp.zeros_like(carry_s)

    h = _rms(x_ref[...], g_ref[...])
    h_ref[...] = h
    logits = jnp.dot(h, wr_ref[...], precision=HIGHEST, preferred_element_type=F32)
    lane = lax.broadcasted_iota(jnp.int32, (tm, LANES), 1).astype(F32)
    lg = jnp.where(lane < N_EXPERTS, logits, -jnp.inf)

    def take_top(v):
        top = jnp.max(v, axis=-1, keepdims=True)
        idx = jnp.min(jnp.where(v == top, lane, float(LANES)), axis=-1, keepdims=True)
        return top, idx

    m1, i1 = take_top(lg)
    m2, i2 = take_top(jnp.where(lane == i1, -jnp.inf, lg))
    e = jnp.exp(m2 - m1)
    w1 = 1.0 / (1.0 + e)
    w2 = e / (1.0 + e)
    oh1 = lane == i1
    oh2 = lane == i2
    oh = jnp.where(oh1 | oh2, 1.0, 0.0)
    r = lax.broadcasted_iota(jnp.int32, (tm, tm), 0)
    c = lax.broadcasted_iota(jnp.int32, (tm, tm), 1)
    lower = jnp.where(c < r, 1.0, 0.0).astype(BF16)
    before = _dot(lower, oh.astype(BF16)) + carry_s[...]
    rank1 = jnp.sum(jnp.where(oh1, before, 0.0), axis=-1, keepdims=True)
    rank2 = jnp.sum(jnp.where(oh2, before, 0.0), axis=-1, keepdims=True)
    carry_s[...] += jnp.sum(oh, axis=0, keepdims=True)
    meta = jnp.zeros((tm, LANES), F32)
    for k, v in ((META_E, i1), (META_E + 1, i2), (META_W, w1), (META_W + 1, w2),
                 (META_RANK, rank1), (META_RANK + 1, rank2)):
        meta = jnp.where(lane == k, v, meta)
    meta_ref[...] = meta
    cnt_ref[...] = jnp.broadcast_to(carry_s[...], cnt_ref.shape)


def moe_router(x, g, w_router, *, tm=512):
    t, d = x.shape
    tm = min(tm, t)
    wr = jnp.pad(w_router, ((0, 0), (0, LANES - N_EXPERTS)))
    return pl.pallas_call(
        functools.partial(_router_kernel, tm=tm),
        grid=(t // tm,),
        in_specs=[pl.BlockSpec((tm, d), lambda i: (i, 0)),
                  pl.BlockSpec((1, d), lambda i: (0, 0)),
                  pl.BlockSpec((d, LANES), lambda i: (0, 0))],
        out_specs=[pl.BlockSpec((tm, d), lambda i: (i, 0)),
                   pl.BlockSpec((tm, LANES), lambda i: (i, 0)),
                   pl.BlockSpec((8, LANES), lambda i: (0, 0))],
        out_shape=[jax.ShapeDtypeStruct((t, d), F32),
                   jax.ShapeDtypeStruct((t, LANES), F32),
                   jax.ShapeDtypeStruct((8, LANES), F32)],
        scratch_shapes=[pltpu.VMEM((1, LANES), F32)],
        compiler_params=_cparams(("arbitrary",)),
        name="moe_router",
    )(x, g.reshape(1, d), wr)


def _dispatch_kernel(p1_ref, p2_ref, h_ref, init_ref, xs_ref, sem, *, tr):
    del init_ref
    base = pl.program_id(0) * tr

    def copies(r, tok):
        src = h_ref.at[pl.ds(r, 1)]
        return (pltpu.make_async_copy(src, xs_ref.at[pl.ds(p1_ref[tok], 1)], sem.at[0]),
                pltpu.make_async_copy(src, xs_ref.at[pl.ds(p2_ref[tok], 1)], sem.at[1]))

    def issue(r, carry):
        for c in copies(r, base + r):
            c.start()
        return carry

    lax.fori_loop(0, tr, issue, 0)

    def drain(r, carry):
        for c in copies(0, 0):
            c.wait()
        return carry

    lax.fori_loop(0, tr, drain, 0)


def moe_dispatch(h, pos1, pos2, m_pad, *, tr=256):
    t, d = h.shape
    tr = min(tr, t)
    grid_spec = pltpu.PrefetchScalarGridSpec(
        num_scalar_prefetch=2,
        grid=(t // tr,),
        in_specs=[pl.BlockSpec((tr, d), lambda i, p1, p2: (i, 0)),
                  pl.BlockSpec(memory_space=pl.ANY)],
        out_specs=pl.BlockSpec(memory_space=pl.ANY),
        scratch_shapes=[pltpu.SemaphoreType.DMA((2,))])
    return pl.pallas_call(
        functools.partial(_dispatch_kernel, tr=tr),
        grid_spec=grid_spec,
        out_shape=jax.ShapeDtypeStruct((m_pad, d), h.dtype),
        input_output_aliases={3: 0},
        compiler_params=_cparams(("arbitrary",)),
        name="moe_dispatch",
    )(pos1, pos2, h, jnp.zeros((m_pad, d), h.dtype))


def _combine_kernel(p1_ref, p2_ref, x_ref, meta_ref, y_ref, o_ref, y1_s, y2_s, sem, *, tr):
    base = pl.program_id(0) * tr

    def copies(r, tok):
        return (pltpu.make_async_copy(y_ref.at[pl.ds(p1_ref[tok], 1)], y1_s.at[pl.ds(r, 1)], sem.at[0]),
                pltpu.make_async_copy(y_ref.at[pl.ds(p2_ref[tok], 1)], y2_s.at[pl.ds(r, 1)], sem.at[1]))

    def issue(r, carry):
        for c in copies(r, base + r):
            c.start()
        return carry

    lax.fori_loop(0, tr, issue, 0)

    def drain(r, carry):
        for c in copies(0, 0):
            c.wait()
        return carry

    lax.fori_loop(0, tr, drain, 0)
    meta = meta_ref[...]
    w1 = meta[:, META_W:META_W + 1]
    w2 = meta[:, META_W + 1:META_W + 2]
    o_ref[...] = x_ref[...] + w1 * y1_s[...] + w2 * y2_s[...]


def moe_combine(x, meta, y, pos1, pos2, *, tr=256):
    t, d = x.shape
    tr = min(tr, t)
    grid_spec = pltpu.PrefetchScalarGridSpec(
        num_scalar_prefetch=2,
        grid=(t // tr,),
        in_specs=[pl.BlockSpec((tr, d), lambda i, p1, p2: (i, 0)),
                  pl.BlockSpec((tr, LANES), lambda i, p1, p2: (i, 0)),
                  pl.BlockSpec(memory_space=pl.ANY)],
        out_specs=pl.BlockSpec((tr, d), lambda i, p1, p2: (i, 0)),
        scratch_shapes=[pltpu.VMEM((tr, d), F32), pltpu.VMEM((tr, d), F32),
                        pltpu.SemaphoreType.DMA((2,))])
    return pl.pallas_call(
        functools.partial(_combine_kernel, tr=tr),
        grid_spec=grid_spec,
        out_shape=jax.ShapeDtypeStruct((t, d), F32),
        compiler_params=_cparams(("arbitrary",)),
        name="moe_combine",
    )(pos1, pos2, x, meta, y)


def moe_ffn(x, ffn_norm, w_router, wg, wu, wd, *, tm=512):
    t, d = x.shape
    tm = min(tm, t)
    h, meta, cnt = moe_router(x, ffn_norm, w_router)
    counts = cnt[0, :N_EXPERTS].astype(jnp.int32)
    padded = (counts + tm - 1) // tm * tm
    ends = jnp.cumsum(padded)
    starts = ends - padded
    ids = meta[:, META_E:META_E + 2].astype(jnp.int32)
    ranks = meta[:, META_RANK:META_RANK + 2].astype(jnp.int32)
    pos = starts[ids] + ranks
    pos1, pos2 = pos[:, 0], pos[:, 1]
    m_pad = TOP_K * t + N_EXPERTS * tm
    n_tiles = m_pad // tm
    tile_start = jnp.arange(n_tiles, dtype=jnp.int32) * tm
    n_used = (ends[-1] // tm).astype(jnp.int32)
    tile_expert = jnp.sum((tile_start[:, None] >= ends[None, :]).astype(jnp.int32), axis=1)
    last_expert = jnp.sum((ends[-1] - 1 >= ends).astype(jnp.int32))
    tile_expert = jnp.where(tile_start < ends[-1], tile_expert, last_expert).astype(jnp.int32)
    xs = moe_dispatch(h, pos1, pos2, m_pad)
    y = grouped_ffn(xs, tile_expert, n_used.reshape(1), wg, wu, wd, tm=tm)
    return moe_combine(x, meta, y, pos1, pos2)


def _compress_kernel(c_ref, pe_ref, w1_ref, w2_ref, ng_ref, o_ref, *, nrows, half, normalize):
    c = c_ref[...].astype(MXU_DTYPE)
    top = _dot(c, w1_ref[0:half, :])
    bot = _dot(c, w1_ref[half:, :])
    bot_next = pltpu.roll(bot, nrows - 1, 0)
    pe_term = _dot(pe_ref[...].astype(MXU_DTYPE), w1_ref[...])
    hid = top + bot_next + pe_term
    out = _dot((hid * _sigmoid(hid)).astype(MXU_DTYPE), w2_ref[...])
    if normalize:
        out = _rms(out, ng_ref[...])
    valid = lax.broadcasted_iota(jnp.int32, (nrows, 1), 0) < nrows - 1
    o_ref[...] = jnp.where(valid, out, 0.0)


def compress(c, pe, w1, w2, norm_g, *, normalize):
    g, nrows, half = c.shape
    return pl.pallas_call(
        functools.partial(_compress_kernel, nrows=nrows, half=half, normalize=normalize),
        grid=(g,),
        in_specs=[pl.BlockSpec((None, nrows, half), lambda i: (i, 0, 0)),
                  pl.BlockSpec((1, 2 * half), lambda i: (0, 0)),
                  pl.BlockSpec((2 * half, CMP_HIDDEN), lambda i: (0, 0)),
                  pl.BlockSpec((CMP_HIDDEN, HEAD_DIM), lambda i: (0, 0)),
                  pl.BlockSpec((1, HEAD_DIM), lambda i: (0, 0))],
        out_specs=pl.BlockSpec((None, nrows, HEAD_DIM), lambda i: (i, 0, 0)),
        out_shape=jax.ShapeDtypeStruct((g, nrows, HEAD_DIM), F32),
        compiler_params=_cparams(("arbitrary",)),
        name="nsa_compress",
    )(c, pe.reshape(1, 2 * half), w1, w2, norm_g.reshape(1, HEAD_DIM))


def _nsa_cmp_kernel(slopes_ref, q_ref, kc_ref, vc_ref, qg_ref, ov_ref, oc_ref, sel_ref, used_ref, *, tq, ncmp, nsel):
    g = pl.program_id(0)
    qi = pl.program_id(1)
    kc = kc_ref[...].astype(MXU_DTYPE)
    vc = vc_ref[...].astype(MXU_DTYPE)
    tpos = qi * tq + lax.broadcasted_iota(jnp.int32, (tq, 1), 0)
    cend = lax.broadcasted_iota(jnp.int32, (1, ncmp), 1) * CMP_STRIDE + (CMP_LEN - 1)
    dist = tpos - cend
    ok = dist >= 0
    dist_f = dist.astype(F32)
    any_ok = jnp.where(tpos >= CMP_LEN - 1, 1.0, 0.0)
    psum = jnp.zeros((tq, ncmp), F32)
    for r in range(NSA_GROUP):
        cols = slice(r * HEAD_DIM, (r + 1) * HEAD_DIM)
        qn = _rms(q_ref[:, cols], qg_ref[...])
        s = _dot_nt((qn * HEAD_DIM ** -0.5).astype(MXU_DTYPE), kc)
        s = jnp.where(ok, s - slopes_ref[g * NSA_GROUP + r] * dist_f, NEG)
        p = jnp.exp(s - jnp.max(s, axis=-1, keepdims=True))
        p = p * (any_ok / jnp.sum(p, axis=-1, keepdims=True))
        oc_ref[:, cols] = _dot(p.astype(MXU_DTYPE), vc).astype(oc_ref.dtype)
        psum = psum + p
    imp = _dot_nt(ov_ref[...], psum, precision=HIGHEST)
    blk = lax.broadcasted_iota(jnp.int32, (LANES, tq), 0)
    qb = (qi * tq + lax.broadcasted_iota(jnp.int32, (LANES, tq), 1)) // SLC_BLOCK
    forced = (blk == 0) | (blk == qb) | (blk == qb - 1)
    visible = blk <= qb
    imp = jnp.where(forced, jnp.inf, jnp.where(visible, imp, -jnp.inf))
    blk_f = blk.astype(F32)
    sel = jnp.zeros((LANES, tq), jnp.bool_)
    for _ in range(min(SLC_TOPN, nsel)):
        top = jnp.max(imp, axis=0, keepdims=True)
        first = jnp.min(jnp.where(imp == top, blk_f, float(LANES)), axis=0, keepdims=True)
        pick = blk_f == first
        sel = sel | pick
        imp = jnp.where(pick, -jnp.inf, imp)
    sel_q = jnp.where(sel & visible, 1.0, 0.0).T
    sel_ref[...] = sel_q.astype(sel_ref.dtype)
    used_ref[...] = jnp.broadcast_to(jnp.max(sel_q, axis=0, keepdims=True), used_ref.shape)


def nsa_compressed(u, k_cmp, v_cmp, q_norm, *, tq=256):
    t = u.shape[0]
    tq = min(tq, t)
    ncmp = k_cmp.shape[1]
    nsel = t // SLC_BLOCK
    assert nsel <= LANES
    nstart = np.arange(ncmp)[:, None] * CMP_STRIDE
    lo = np.arange(LANES)[None, :] * SLC_BLOCK
    overlap = ((nstart < lo + SLC_BLOCK) & (nstart + CMP_LEN > lo) & (np.arange(ncmp)[:, None] < ncmp - 1))
    ov_t = jnp.asarray(overlap.T.astype(np.float32))
    gw = NSA_GROUP * HEAD_DIM
    grid_spec = pltpu.PrefetchScalarGridSpec(
        num_scalar_prefetch=1,
        grid=(NSA_KV_HEADS, t // tq),
        in_specs=[pl.BlockSpec((tq, gw), lambda g, i, s: (i, g)),
                  pl.BlockSpec((None, ncmp, HEAD_DIM), lambda g, i, s: (g, 0, 0)),
                  pl.BlockSpec((None, ncmp, HEAD_DIM), lambda g, i, s: (g, 0, 0)),
                  pl.BlockSpec((1, HEAD_DIM), lambda g, i, s: (0, 0)),
                  pl.BlockSpec((LANES, ncmp), lambda g, i, s: (0, 0))],
        out_specs=[pl.BlockSpec((tq, gw), lambda g, i, s: (i, g)),
                   pl.BlockSpec((None, tq, LANES), lambda g, i, s: (g, i, 0)),
                   pl.BlockSpec((None, None, 8, LANES), lambda g, i, s: (g, i, 0, 0))])
    return pl.pallas_call(
        functools.partial(_nsa_cmp_kernel, tq=tq, ncmp=ncmp, nsel=nsel),
        grid_spec=grid_spec,
        out_shape=[jax.ShapeDtypeStruct((t, NSA_DIM), F32),
                   jax.ShapeDtypeStruct((NSA_KV_HEADS, t, LANES), MXU_DTYPE),
                   jax.ShapeDtypeStruct((NSA_KV_HEADS, t // tq, 8, LANES), F32)],
        compiler_params=_cparams(("arbitrary", "arbitrary")),
        name="nsa_compressed",
    )(_nsa_slopes(), u, k_cmp, v_cmp, q_norm.reshape(1, HEAD_DIM), ov_t)


def _nsa_slopes():
    return jnp.asarray(2.0 ** (-8.0 * np.arange(1, NSA_HEADS + 1) / NSA_HEADS), dtype=F32)


def _group_queries(q_ref, qg_ref, qaug_s, tq):
    for r in range(NSA_GROUP):
        qn = _rms(q_ref[:, r * HEAD_DIM:(r + 1) * HEAD_DIM], qg_ref[...])
        qaug_s[r * tq:(r + 1) * tq, 0:HEAD_DIM] = (qn * QK_SCALE_LOG2).astype(qaug_s.dtype)


def _nsa_slc_kernel(slopes_ref, tiles_ref, npast_ref, q_ref, k_ref, v_ref, sel_ref, qg_ref, kg_ref, o_ref,
                    kaug_s, vt_s, qaug_s, sb_s, s_s, m_s, acc_s, *, tq, t_total):
    g = pl.program_id(0)
    qi = pl.program_id(1)
    tk = tq
    row_len = t_total // tk + 1
    slopes = [slopes_ref[g * NSA_GROUP + r] for r in range(NSA_GROUP)]

    @pl.when(qi == 0)
    def _():
        _kv_prologue(k_ref, v_ref, kg_ref, kaug_s, vt_s, t=t_total, tk=tk, blk=SLC_BLOCK)
        _fill_bias(sb_s, slopes, (PLAIN, CAUSAL), tk=tk, tq=tq)

    _group_queries(q_ref, qg_ref, qaug_s, tq)
    sel_bias = jnp.where(sel_ref[...].astype(F32) > 0.5, 0.0, NEG).astype(qaug_s.dtype)
    for r in range(NSA_GROUP):
        qaug_s[r * tq:(r + 1) * tq, HEAD_DIM:] = sel_bias
    _flash_init(m_s, acc_s)
    qk, spv = _flash_stages(t0=qi * tq, slopes=slopes, qaug_s=qaug_s, kaug_s=kaug_s, vt_s=vt_s, sb_s=sb_s,
                            s_s=s_s, m_s=m_s, acc_s=acc_s, tq=tq, tk=tk)
    row = g * pl.num_programs(1) + qi
    _causal_sweep(qk, spv, lambda k: tiles_ref[row * row_len + k], npast_ref[row])
    for r in range(NSA_GROUP):
        o_ref[:, r * HEAD_DIM:(r + 1) * HEAD_DIM] = _flash_result(acc_s, r, tq).astype(o_ref.dtype)


def _nsa_win_kernel(slopes_ref, q_ref, k_ref, v_ref, oc_ref, os_ref, gl_ref, qg_ref, kg_ref, o_ref,
                    kaug_s, vt_s, qaug_s, sb_s, s_s, m_s, acc_s, *, tq, t_total):
    g = pl.program_id(0)
    qi = pl.program_id(1)
    tk = tq
    nw = WINDOW // tk
    slopes = [slopes_ref[g * NSA_GROUP + r] for r in range(NSA_GROUP)]

    @pl.when(qi == 0)
    def _():
        _kv_prologue(k_ref, v_ref, kg_ref, kaug_s, vt_s, t=t_total, tk=tk, blk=None)
        _fill_bias(sb_s, slopes, (PLAIN, CAUSAL, WINDOW_TAIL), tk=tk, tq=tq)

    _group_queries(q_ref, qg_ref, qaug_s, tq)
    _flash_init(m_s, acc_s)
    qk, spv = _flash_stages(t0=qi * tq, slopes=slopes, qaug_s=qaug_s, kaug_s=kaug_s, vt_s=vt_s, sb_s=sb_s,
                            s_s=s_s, m_s=m_s, acc_s=acc_s, tq=tq, tk=tk)
    kinds = [WINDOW_TAIL] + [PLAIN] * (nw - 1) + [CAUSAL]

    @pl.when(qi >= nw)
    def _():
        for i in range(nw + 1):
            qk(qi - nw + i, i)
        for i, kind in enumerate(kinds):
            spv(qi - nw + i, i, kind)

    @pl.when(qi < nw)
    def _():
        for back in range(nw - 1, -1, -1):
            @pl.when(qi >= back)
            def _():
                qk(qi - back, 0)
                spv(qi - back, 0, kinds[nw - back])
    gates = _sigmoid(gl_ref[...])
    for r in range(NSA_GROUP):
        cols = slice(r * HEAD_DIM, (r + 1) * HEAD_DIM)
        lane0 = 3 * r
        o = (gates[:, lane0:lane0 + 1] * oc_ref[:, cols] + gates[:, lane0 + 1:lane0 + 2] * os_ref[:, cols]
             + gates[:, lane0 + 2:lane0 + 3] * _flash_result(acc_s, r, tq))
        o_ref[:, cols] = o.astype(o_ref.dtype)


def nsa_selected(u, sel, blocks_used, q_norm, k_norm, *, kcol, vcol, tq=256):
    t = u.shape[0]
    tq = min(tq, t)
    gw = NSA_GROUP * HEAD_DIM
    per_tile = tq // SLC_BLOCK
    nt = t // tq
    used = blocks_used[:, :, 0, :t // SLC_BLOCK].reshape(NSA_KV_HEADS, nt, nt, per_tile)
    tile_ids = jnp.arange(nt, dtype=jnp.int32)
    past_used = (jnp.max(used, axis=-1) > 0) & (tile_ids[None, None, :] < tile_ids[None, :, None])
    n_past = jnp.sum(past_used, axis=-1).astype(jnp.int32)
    order = jnp.argsort(jnp.logical_not(past_used), axis=-1, stable=True).astype(jnp.int32)
    slots = jnp.arange(nt + 1, dtype=jnp.int32)
    order = jnp.concatenate([order, order[..., :1]], axis=-1)
    tiles = jnp.where(slots < n_past[..., None], order, tile_ids[None, :, None])
    grid_spec = pltpu.PrefetchScalarGridSpec(
        num_scalar_prefetch=3,
        grid=(NSA_KV_HEADS, nt),
        in_specs=[pl.BlockSpec((tq, gw), lambda g, i, s, tl, n: (i, g)),
                  pl.BlockSpec((t, HEAD_DIM), lambda g, i, s, tl, n: (0, kcol + g)),
                  pl.BlockSpec((t, HEAD_DIM), lambda g, i, s, tl, n: (0, vcol + g)),
                  pl.BlockSpec((None, tq, LANES), lambda g, i, s, tl, n: (g, i, 0)),
                  pl.BlockSpec((1, HEAD_DIM), lambda g, i, s, tl, n: (0, 0)),
                  pl.BlockSpec((1, HEAD_DIM), lambda g, i, s, tl, n: (0, 0))],
        out_specs=pl.BlockSpec((tq, gw), lambda g, i, s, tl, n: (i, g)),
        scratch_shapes=_flash_scratch(t, tq, tq, NSA_GROUP, HEAD_DIM + LANES, 2, 2))
    return pl.pallas_call(
        functools.partial(_nsa_slc_kernel, tq=tq, t_total=t),
        grid_spec=grid_spec,
        out_shape=jax.ShapeDtypeStruct((t, NSA_DIM), F32),
        compiler_params=_cparams(("arbitrary", "arbitrary")),
        name="nsa_selected",
    )(_nsa_slopes(), tiles.reshape(-1), n_past.reshape(-1), u, u, u, sel,
      q_norm.reshape(1, HEAD_DIM), k_norm.reshape(1, HEAD_DIM))


def nsa_window_merge(u, o_cmp, o_slc, gate_logits, q_norm, k_norm, *, kcol, vcol, tq=256):
    t = u.shape[0]
    tq = min(tq, t)
    assert WINDOW % tq == 0
    gw = NSA_GROUP * HEAD_DIM
    gl = gate_logits.reshape(t, NSA_KV_HEADS, NSA_GROUP * 3)
    gl = jnp.pad(gl, ((0, 0), (0, 0), (0, LANES - NSA_GROUP * 3))).reshape(t, NSA_KV_HEADS * LANES)
    grid_spec = pltpu.PrefetchScalarGridSpec(
        num_scalar_prefetch=1,
        grid=(NSA_KV_HEADS, t // tq),
        in_specs=[pl.BlockSpec((tq, gw), lambda g, i, s: (i, g)),
                  pl.BlockSpec((t, HEAD_DIM), lambda g, i, s: (0, kcol + g)),
                  pl.BlockSpec((t, HEAD_DIM), lambda g, i, s: (0, vcol + g)),
                  pl.BlockSpec((tq, gw), lambda g, i, s: (i, g)),
                  pl.BlockSpec((tq, gw), lambda g, i, s: (i, g)),
                  pl.BlockSpec((tq, LANES), lambda g, i, s: (i, g)),
                  pl.BlockSpec((1, HEAD_DIM), lambda g, i, s: (0, 0)),
                  pl.BlockSpec((1, HEAD_DIM), lambda g, i, s: (0, 0))],
        out_specs=pl.BlockSpec((tq, gw), lambda g, i, s: (i, g)),
        scratch_shapes=_flash_scratch(t, tq, tq, NSA_GROUP, HEAD_DIM, 3, WINDOW // tq + 1))
    return pl.pallas_call(
        functools.partial(_nsa_win_kernel, tq=tq, t_total=t),
        grid_spec=grid_spec,
        out_shape=jax.ShapeDtypeStruct((t, NSA_DIM), MXU_DTYPE),
        compiler_params=_cparams(("arbitrary", "arbitrary")),
        name="nsa_window_merge",
    )(_nsa_slopes(), u, u, u, o_cmp, o_slc, gl, q_norm.reshape(1, HEAD_DIM), k_norm.reshape(1, HEAD_DIM))


def _mx(w):
    return w.astype(MXU_DTYPE)


def conv_moba_layer(x, attn_norm, w_in, conv_w, conv_b, conv_norm_g, conv_norm_b, q_norm, k_norm,
                    w_out, ffn_norm, w_gate, w_up, w_down):
    u = norm_matmul(x, attn_norm, _mx(w_in))
    a = conv_module(u, conv_w, conv_b, conv_norm_g, conv_norm_b)
    o = moba_attention(u, q_norm, k_norm, col0=2 * CONV_CH // HEAD_DIM)
    w_out = _mx(w_out)
    x = proj_residual(x, [(a, w_out[:CONV_CH]), (o, w_out[CONV_CH:])])
    return dense_ffn(x, ffn_norm, _mx(w_gate), _mx(w_up), _mx(w_down))


def nsa_moe_layer(x, attn_norm, w_in, q_norm, kc_norm, ks_norm, kw_norm, cmp_pe_k, cmp_w1_k, cmp_w2_k,
                  cmp_pe_v, cmp_w1_v, cmp_w2_v, w_out, ffn_norm, w_router, w_gate, w_up, w_down):
    t = x.shape[0]
    main = NSA_DIM + 6 * NSA_KV_DIM
    n_gate = 3 * NSA_HEADS
    w_in_p = jnp.pad(_mx(w_in), ((0, 0), (0, 1024 - n_gate)))
    u = norm_matmul(x, attn_norm, w_in_p)
    gate_logits = u[:, main:main + n_gate]
    nrows = t // CMP_STRIDE

    def blocks(col):
        v = u[:, col:col + NSA_KV_DIM].reshape(nrows, CMP_STRIDE, NSA_KV_HEADS, HEAD_DIM)
        return v.transpose(2, 0, 1, 3).reshape(NSA_KV_HEADS, nrows, CMP_STRIDE * HEAD_DIM)

    k_cmp = compress(blocks(NSA_DIM), cmp_pe_k, _mx(cmp_w1_k), _mx(cmp_w2_k), kc_norm, normalize=True)
    v_cmp = compress(blocks(NSA_DIM + NSA_KV_DIM), cmp_pe_v, _mx(cmp_w1_v), _mx(cmp_w2_v), kc_norm,
                     normalize=False)
    o_cmp, sel, blocks_used = nsa_compressed(u, k_cmp, v_cmp, q_norm)
    cb = lambda col: col // HEAD_DIM
    o_slc = nsa_selected(u, sel, blocks_used, q_norm, ks_norm, kcol=cb(NSA_DIM + 2 * NSA_KV_DIM),
                         vcol=cb(NSA_DIM + 3 * NSA_KV_DIM))
    o = nsa_window_merge(u, o_cmp, o_slc, gate_logits, q_norm, kw_norm,
                         kcol=cb(NSA_DIM + 4 * NSA_KV_DIM), vcol=cb(NSA_DIM + 5 * NSA_KV_DIM))
    x = proj_residual(x, [(o, _mx(w_out))])
    return moe_ffn(x, ffn_norm, w_router, _mx(w_gate), _mx(w_up), _mx(w_down))


def kernel(x, l0_attn_norm, l0_w_in, l0_conv_w, l0_conv_b, l0_conv_norm_g, l0_conv_norm_b, l0_q_norm, l0_k_norm, l0_w_out, l0_ffn_norm, l0_w_gate, l0_w_up, l0_w_down, l1_attn_norm, l1_w_in, l1_q_norm, l1_kc_norm, l1_ks_norm, l1_kw_norm, l1_cmp_pe_k, l1_cmp_w1_k, l1_cmp_w2_k, l1_cmp_pe_v, l1_cmp_w1_v, l1_cmp_w2_v, l1_w_out, l1_ffn_norm, l1_w_router, l1_w_gate, l1_w_up, l1_w_down, l2_attn_norm, l2_w_in, l2_conv_w, l2_conv_b, l2_conv_norm_g, l2_conv_norm_b, l2_q_norm, l2_k_norm, l2_w_out, l2_ffn_norm, l2_w_gate, l2_w_up, l2_w_down, l3_attn_norm, l3_w_in, l3_q_norm, l3_kc_norm, l3_ks_norm, l3_kw_norm, l3_cmp_pe_k, l3_cmp_w1_k, l3_cmp_w2_k, l3_cmp_pe_v, l3_cmp_w1_v, l3_cmp_w2_v, l3_w_out, l3_ffn_norm, l3_w_router, l3_w_gate, l3_w_up, l3_w_down):
    b, t, d = x.shape
    assert b == 1 and d == D_MODEL
    h = x.reshape(t, d)
    h = conv_moba_layer(h, l0_attn_norm, l0_w_in, l0_conv_w, l0_conv_b, l0_conv_norm_g, l0_conv_norm_b, l0_q_norm, l0_k_norm, l0_w_out, l0_ffn_norm, l0_w_gate, l0_w_up, l0_w_down)
    h = nsa_moe_layer(h, l1_attn_norm, l1_w_in, l1_q_norm, l1_kc_norm, l1_ks_norm, l1_kw_norm, l1_cmp_pe_k, l1_cmp_w1_k, l1_cmp_w2_k, l1_cmp_pe_v, l1_cmp_w1_v, l1_cmp_w2_v, l1_w_out, l1_ffn_norm, l1_w_router, l1_w_gate, l1_w_up, l1_w_down)
    h = conv_moba_layer(h, l2_attn_norm, l2_w_in, l2_conv_w, l2_conv_b, l2_conv_norm_g, l2_conv_norm_b, l2_q_norm, l2_k_norm, l2_w_out, l2_ffn_norm, l2_w_gate, l2_w_up, l2_w_down)
    h = nsa_moe_layer(h, l3_attn_norm, l3_w_in, l3_q_norm, l3_kc_norm, l3_ks_norm, l3_kw_norm, l3_cmp_pe_k, l3_cmp_w1_k, l3_cmp_w2_k, l3_cmp_pe_v, l3_cmp_w1_v, l3_cmp_w2_v, l3_w_out, l3_ffn_norm, l3_w_router, l3_w_gate, l3_w_up, l3_w_down)
    return h.reshape(b, t, d)
```

```python
import functools

import numpy as np
import jax
import jax.numpy as jnp
from jax import lax
from jax.experimental import pallas as pl
from jax.experimental.pallas import tpu as pltpu

D_MODEL = 2048
HEAD_DIM = 128
CONV_CH = D_MODEL // 2
CONV_WIDTH = 31
MOBA_HEADS = (D_MODEL // 2) // HEAD_DIM
MOBA_DIM = MOBA_HEADS * HEAD_DIM
MOBA_BLOCK = 256
MOBA_TOPK = 3
NSA_HEADS = D_MODEL // HEAD_DIM
NSA_KV_HEADS = NSA_HEADS // 4
NSA_GROUP = NSA_HEADS // NSA_KV_HEADS
NSA_DIM = NSA_HEADS * HEAD_DIM
NSA_KV_DIM = NSA_KV_HEADS * HEAD_DIM
CMP_LEN = 32
CMP_STRIDE = 16
CMP_HIDDEN = 256
SLC_BLOCK = 64
SLC_TOPN = 16
WINDOW = 512
D_FF = ((8 * D_MODEL) // 3 + 255) // 256 * 256
N_EXPERTS = 8
TOP_K = 2
EPS = 1e-6
NEG = -1e30

LANES = 128
F32 = jnp.float32
BF16 = jnp.bfloat16
MXU_DTYPE = BF16
HIGHEST = lax.Precision.HIGHEST
VMEM_LIMIT = 56 * 1024 * 1024


def _cparams(sem):
    return pltpu.CompilerParams(dimension_semantics=sem, vmem_limit_bytes=VMEM_LIMIT)


def _rms(x, g):
    return x * lax.rsqrt(jnp.mean(x * x, axis=-1, keepdims=True) + EPS) * g


def _dot(a, b):
    return jnp.dot(a, b, preferred_element_type=F32)


def _dot_nt(a, b, precision=None):
    return lax.dot_general(a, b, (((1,), (1,)), ((), ())), precision=precision,
                           preferred_element_type=F32)


def _sigmoid(x):
    return 1.0 / (1.0 + jnp.exp(-x))


def _norm_matmul_kernel(x_ref, g_ref, w_ref, o_ref, h_ref):
    @pl.when(pl.program_id(1) == 0)
    def _():
        h_ref[...] = _rms(x_ref[...], g_ref[...]).astype(h_ref.dtype)

    o_ref[...] = _dot(h_ref[...], w_ref[...]).astype(o_ref.dtype)


def norm_matmul(x, g, w, *, tm=1024, tn=1024):
    t, d = x.shape
    n = w.shape[1]
    tm = min(tm, t)
    assert t % tm == 0 and n % tn == 0
    return pl.pallas_call(
        _norm_matmul_kernel,
        grid=(t // tm, n // tn),
        in_specs=[pl.BlockSpec((tm, d), lambda i, j: (i, 0)),
                  pl.BlockSpec((1, d), lambda i, j: (0, 0)),
                  pl.BlockSpec((d, tn), lambda i, j: (0, j))],
        out_specs=pl.BlockSpec((tm, tn), lambda i, j: (i, j)),
        out_shape=jax.ShapeDtypeStruct((t, n), F32),
        scratch_shapes=[pltpu.VMEM((tm, d), MXU_DTYPE)],
        compiler_params=_cparams(("arbitrary", "arbitrary")),
        name="norm_matmul",
    )(x, g.reshape(1, d), w)


def _proj_residual_kernel(*refs, n_pairs):
    x_ref = refs[0]
    o_ref = refs[1 + 2 * n_pairs]
    acc = x_ref[...]
    for p in range(n_pairs):
        acc = acc + _dot(refs[1 + 2 * p][...], refs[2 + 2 * p][...])
    o_ref[...] = acc


def proj_residual(x, pairs, *, tm=1024, tn=1024):
    t, d = x.shape
    tm = min(tm, t)
    in_specs = [pl.BlockSpec((tm, tn), lambda i, j: (i, j))]
    args = [x]
    for a, w in pairs:
        k = a.shape[1]
        in_specs.append(pl.BlockSpec((tm, k), lambda i, j: (i, 0)))
        in_specs.append(pl.BlockSpec((k, tn), lambda i, j: (0, j)))
        args += [a, w]
    return pl.pallas_call(
        functools.partial(_proj_residual_kernel, n_pairs=len(pairs)),
        grid=(t // tm, d // tn),
        in_specs=in_specs,
        out_specs=pl.BlockSpec((tm, tn), lambda i, j: (i, j)),
        out_shape=jax.ShapeDtypeStruct((t, d), F32),
        compiler_params=_cparams(("arbitrary", "arbitrary")),
        name="proj_residual",
    )(*args)


CONV_HALO = 32
CONV_ROWS = 32


def _conv_kernel(av_ref, ag_ref, hv_ref, hg_ref, w_ref, b_ref, lg_ref, lb_ref, o_ref, s_ref, *, tq):
    i = pl.program_id(0)
    halo = hv_ref[...] * _sigmoid(hg_ref[...])
    s_ref[0:CONV_HALO, :] = jnp.where(i > 0, halo, 0.0)
    s_ref[CONV_HALO:, :] = av_ref[...] * _sigmoid(ag_ref[...])
    first = CONV_HALO - (CONV_WIDTH - 1)
    for c in range(tq // CONV_ROWS):
        base = c * CONV_ROWS + first
        acc = jnp.broadcast_to(b_ref[...], (CONV_ROWS, CONV_CH))
        for k in range(CONV_WIDTH):
            acc = acc + s_ref[base + k:base + k + CONV_ROWS, :] * w_ref[k:k + 1, :]
        mu = jnp.mean(acc, axis=-1, keepdims=True)
        cen = acc - mu
        var = jnp.mean(cen * cen, axis=-1, keepdims=True)
        y = cen * lax.rsqrt(var + EPS) * lg_ref[...] + lb_ref[...]
        o_ref[c * CONV_ROWS:(c + 1) * CONV_ROWS, :] = (y * _sigmoid(y)).astype(o_ref.dtype)


def conv_module(u, conv_w, conv_b, ln_g, ln_b, *, tq=512):
    t = u.shape[0]
    tq = min(tq, t)
    c = CONV_CH
    hb = tq // CONV_HALO
    w = jnp.pad(conv_w.reshape(CONV_WIDTH, c), ((0, 1), (0, 0)))
    row = lambda v: v.reshape(1, c)
    const = lambda i: (0, 0)
    return pl.pallas_call(
        functools.partial(_conv_kernel, tq=tq),
        grid=(t // tq,),
        in_specs=[pl.BlockSpec((tq, c), lambda i: (i, 0)),
                  pl.BlockSpec((tq, c), lambda i: (i, 1)),
                  pl.BlockSpec((CONV_HALO, c), lambda i: (jnp.maximum(i * hb - 1, 0), 0)),
                  pl.BlockSpec((CONV_HALO, c), lambda i: (jnp.maximum(i * hb - 1, 0), 1)),
                  pl.BlockSpec((CONV_WIDTH + 1, c), const),
                  pl.BlockSpec((1, c), const), pl.BlockSpec((1, c), const), pl.BlockSpec((1, c), const)],
        out_specs=pl.BlockSpec((tq, c), lambda i: (i, 0)),
        out_shape=jax.ShapeDtypeStruct((t, c), MXU_DTYPE),
        scratch_shapes=[pltpu.VMEM((tq + CONV_HALO, c), F32)],
        compiler_params=_cparams(("arbitrary",)),
        name="conv_module",
    )(u, u, u, u, w, row(conv_b), row(ln_g), row(ln_b))


ONES_ROWS = 16
LOG2E = 1.4426950408889634
QK_SCALE_LOG2 = HEAD_DIM ** -0.5 * LOG2E


def _kv_prologue(k_ref, v_ref, kg_ref, kaug_s, vt_s, *, t, tk, blk):
    kn = _rms(k_ref[...], kg_ref[...])
    kaug_s[:, 0:HEAD_DIM] = kn.astype(kaug_s.dtype)
    if blk is not None:
        lane_blk = lax.broadcasted_iota(jnp.int32, (t, LANES), 1)
        key_blk = lax.broadcasted_iota(jnp.int32, (t, LANES), 0) // blk
        kaug_s[:, HEAD_DIM:] = jnp.where(lane_blk == key_blk, 1.0, 0.0).astype(kaug_s.dtype)
    for c in range(t // tk):
        vt_s[c, 0:HEAD_DIM, :] = v_ref[c * tk:(c + 1) * tk, :].T.astype(vt_s.dtype)
        vt_s[c, HEAD_DIM:, :] = jnp.ones((ONES_ROWS, tk), vt_s.dtype)
    return kn


PLAIN, CAUSAL, WINDOW_TAIL = 0, 1, 2


def _tile_pattern(kind, tk, tq):
    key_i = lax.broadcasted_iota(jnp.int32, (tk, tq), 0)
    qry_i = lax.broadcasted_iota(jnp.int32, (tk, tq), 1)
    if kind == CAUSAL:
        return key_i <= qry_i
    if kind == WINDOW_TAIL:
        return key_i > qry_i
    return None


def _fill_bias(sb_s, slopes, kinds, *, tk, tq):
    key_off = lax.broadcasted_iota(jnp.int32, (tk, tq), 0).astype(F32)
    for r, slope in enumerate(slopes):
        bias = (LOG2E * slope) * key_off
        for v, kind in enumerate(kinds):
            pattern = _tile_pattern(kind, tk, tq)
            sb_s[v, r] = bias if pattern is None else jnp.where(pattern, bias, NEG)


def _flash_init(m_s, acc_s):
    m_s[...] = jnp.full_like(m_s, NEG)
    acc_s[...] = jnp.zeros_like(acc_s)


def _qk_tile(j, slot, *, heads, qaug_s, kaug_s, s_s, tq, tk):
    kj = kaug_s[pl.ds(pl.multiple_of(j * tk, tk), tk), :]
    for r in range(heads):
        s_s[slot, r] = _dot_nt(kj, qaug_s[r * tq:(r + 1) * tq, :])


def _softmax_pv_tile(j, slot, variant, *, t0, slopes, vt_s, s_s, sb_s, m_s, acc_s, tq, tk):
    vtj = vt_s[j]
    off = (j * tk - t0).astype(F32)
    for r, slope in enumerate(slopes):
        cols = slice(r * tq, (r + 1) * tq)
        s_r = s_s[slot, r] + sb_s[variant, r]
        c = (LOG2E * slope) * off
        m_old = m_s[:, cols]
        m_new = jnp.maximum(m_old, jnp.max(s_r, axis=0, keepdims=True) + c)
        p_r = jnp.exp2(s_r - (m_new - c)).astype(MXU_DTYPE)
        acc_s[:, cols] = jnp.exp2(m_old - m_new) * acc_s[:, cols] + _dot(vtj, p_r)
        m_s[:, cols] = m_new


def _causal_sweep(qk, spv, tile_at, n_past):
    qk(tile_at(0), 0)

    def pair(i, carry):
        k = 2 * i
        qk(tile_at(k + 1), 1)
        spv(tile_at(k), 0, PLAIN)
        qk(tile_at(k + 2), 0)
        spv(tile_at(k + 1), 1, PLAIN)
        return carry

    lax.fori_loop(0, n_past // 2, pair, 0)
    k = (n_past // 2) * 2

    @pl.when(n_past % 2 == 1)
    def _():
        qk(tile_at(k + 1), 1)
        spv(tile_at(k), 0, PLAIN)
        spv(tile_at(k + 1), 1, CAUSAL)

    @pl.when(n_past % 2 == 0)
    def _():
        spv(tile_at(k), 0, CAUSAL)


def _flash_result(acc_s, r, tq):
    cols = slice(r * tq, (r + 1) * tq)
    o_t = acc_s[0:HEAD_DIM, cols] / acc_s[HEAD_DIM:HEAD_DIM + 1, cols]
    return o_t.T


def _flash_scratch(t, tq, tk, heads, kaug_cols, n_patterns, n_slots):
    nq = heads * tq
    return [pltpu.VMEM((t, kaug_cols), MXU_DTYPE),
            pltpu.VMEM((t // tk, HEAD_DIM + ONES_ROWS, tk), MXU_DTYPE),
            pltpu.VMEM((nq, kaug_cols), MXU_DTYPE),
            pltpu.VMEM((n_patterns, heads, tk, tq), F32),
            pltpu.VMEM((n_slots, heads, tk, tq), F32),
            pltpu.VMEM((1, nq), F32),
            pltpu.VMEM((HEAD_DIM + ONES_ROWS, nq), F32)]


def _flash_stages(*, t0, slopes, qaug_s, kaug_s, vt_s, sb_s, s_s, m_s, acc_s, tq, tk):
    qk = functools.partial(_qk_tile, heads=len(slopes), qaug_s=qaug_s, kaug_s=kaug_s, s_s=s_s, tq=tq, tk=tk)
    spv = functools.partial(_softmax_pv_tile, t0=t0, slopes=slopes, vt_s=vt_s, s_s=s_s, sb_s=sb_s, m_s=m_s,
                            acc_s=acc_s, tq=tq, tk=tk)
    return qk, spv


def _moba_kernel(slopes_ref, q_ref, k_ref, v_ref, qg_ref, kg_ref, o_ref,
                 kaug_s, vt_s, qaug_s, sb_s, s_s, m_s, acc_s, km_s, *, tq, t_total):
    h = pl.program_id(0)
    qi = pl.program_id(1)
    nb = t_total // MOBA_BLOCK
    tk = tq

    slopes = [slopes_ref[h]]

    @pl.when(qi == 0)
    def _():
        kn = _kv_prologue(k_ref, v_ref, kg_ref, kaug_s, vt_s, t=t_total, tk=tk, blk=MOBA_BLOCK)
        km_s[...] = jnp.zeros_like(km_s)
        km_s[0:nb, :] = jnp.mean(kn.reshape(nb, MOBA_BLOCK, HEAD_DIM), axis=1)
        _fill_bias(sb_s, slopes, (PLAIN, CAUSAL), tk=tk, tq=tq)

    qn = _rms(q_ref[...], qg_ref[...])
    nb8 = -(-nb // 8) * 8
    gate = _dot_nt(km_s[0:nb8, :], qn, precision=HIGHEST)
    blk = lax.broadcasted_iota(jnp.int32, (nb8, tq), 0)
    blk_f = blk.astype(F32)
    qb = (qi * tq + lax.broadcasted_iota(jnp.int32, (nb8, tq), 1)) // MOBA_BLOCK
    past = blk < qb
    gate = jnp.where(past, gate, -jnp.inf)
    sel = blk == qb
    for _ in range(MOBA_TOPK):
        top = jnp.max(gate, axis=0, keepdims=True)
        first = jnp.min(jnp.where(gate == top, blk_f, float(nb8)), axis=0, keepdims=True)
        pick = blk_f == first
        sel = sel | (pick & past)
        gate = jnp.where(pick, -jnp.inf, gate)
    sel_bias = jnp.where(sel, 0.0, NEG)
    if nb8 < LANES:
        sel_bias = jnp.concatenate([sel_bias, jnp.full((LANES - nb8, tq), NEG, F32)], axis=0)
    qaug_s[:, 0:HEAD_DIM] = (qn * QK_SCALE_LOG2).astype(qaug_s.dtype)
    qaug_s[:, HEAD_DIM:] = sel_bias.T.astype(qaug_s.dtype)

    _flash_init(m_s, acc_s)
    qk, spv = _flash_stages(t0=qi * tq, slopes=slopes, qaug_s=qaug_s, kaug_s=kaug_s, vt_s=vt_s, sb_s=sb_s,
                            s_s=s_s, m_s=m_s, acc_s=acc_s, tq=tq, tk=tk)
    _causal_sweep(qk, spv, lambda k: k, qi)
    o_ref[...] = _flash_result(acc_s, 0, tq).astype(o_ref.dtype)


def moba_attention(u, q_norm, k_norm, *, col0, tq=512):
    t = u.shape[0]
    tq = min(tq, t)
    assert t // MOBA_BLOCK <= LANES and tq % MOBA_BLOCK == 0
    nh = MOBA_HEADS
    slopes = jnp.asarray(2.0 ** (-8.0 * np.arange(1, nh + 1) / nh), dtype=F32)
    grid_spec = pltpu.PrefetchScalarGridSpec(
        num_scalar_prefetch=1,
        grid=(nh, t // tq),
        in_specs=[pl.BlockSpec((tq, HEAD_DIM), lambda h, i, s: (i, col0 + h)),
                  pl.BlockSpec((t, HEAD_DIM), lambda h, i, s: (0, col0 + nh + h)),
                  pl.BlockSpec((t, HEAD_DIM), lambda h, i, s: (0, col0 + 2 * nh + h)),
                  pl.BlockSpec((1, HEAD_DIM), lambda h, i, s: (0, 0)),
                  pl.BlockSpec((1, HEAD_DIM), lambda h, i, s: (0, 0))],
        out_specs=pl.BlockSpec((tq, HEAD_DIM), lambda h, i, s: (i, h)),
        scratch_shapes=_flash_scratch(t, tq, tq, 1, HEAD_DIM + LANES, 2, 2) + [pltpu.VMEM((LANES, HEAD_DIM), F32)])
    return pl.pallas_call(
        functools.partial(_moba_kernel, tq=tq, t_total=t),
        grid_spec=grid_spec,
        out_shape=jax.ShapeDtypeStruct((t, MOBA_DIM), MXU_DTYPE),
        compiler_params=_cparams(("arbitrary", "arbitrary")),
        name="moba_attention",
    )(slopes, u, u, u, q_norm.reshape(1, HEAD_DIM), k_norm.reshape(1, HEAD_DIM))


def _swiglu_step(h, wg_ref, wu_ref, wd_ref):
    a = _dot(h, wg_ref[...])
    b = _dot(h, wu_ref[...])
    mid = (a * _sigmoid(a) * b).astype(MXU_DTYPE)
    return _dot(mid, wd_ref[...])


def _dense_ffn_kernel(x_ref, g_ref, wg_ref, wu_ref, wd_ref, o_ref, h_ref):
    @pl.when(pl.program_id(1) == 0)
    def _():
        x = x_ref[...]
        h_ref[...] = _rms(x, g_ref[...]).astype(h_ref.dtype)
        o_ref[...] = x

    o_ref[...] += _swiglu_step(h_ref[...], wg_ref, wu_ref, wd_ref)


def dense_ffn(x, g, wg, wu, wd, *, tm=512, tf=512):
    t, d = x.shape
    f = wg.shape[1]
    tm = min(tm, t)
    return pl.pallas_call(
        _dense_ffn_kernel,
        grid=(t // tm, f // tf),
        in_specs=[pl.BlockSpec((tm, d), lambda i, j: (i, 0)),
                  pl.BlockSpec((1, d), lambda i, j: (0, 0)),
                  pl.BlockSpec((d, tf), lambda i, j: (0, j)),
                  pl.BlockSpec((d, tf), lambda i, j: (0, j)),
                  pl.BlockSpec((tf, d), lambda i, j: (j, 0))],
        out_specs=pl.BlockSpec((tm, d), lambda i, j: (i, 0)),
        out_shape=jax.ShapeDtypeStruct((t, d), F32),
        scratch_shapes=[pltpu.VMEM((tm, d), MXU_DTYPE)],
        compiler_params=_cparams(("arbitrary", "arbitrary")),
        name="dense_ffn",
    )(x, g.reshape(1, d), wg, wu, wd)


def _grouped_ffn_kernel(te_ref, nu_ref, x_ref, wg_ref, wu_ref, wd_ref, o_ref, h_ref):
    i = pl.program_id(0)
    j = pl.program_id(1)

    @pl.when(j == 0)
    def _():
        h_ref[...] = x_ref[...].astype(h_ref.dtype)
        o_ref[...] = jnp.zeros_like(o_ref)

    @pl.when(i < nu_ref[0])
    def _():
        o_ref[...] += _swiglu_step(h_ref[...], wg_ref, wu_ref, wd_ref)


def grouped_ffn(xs, tile_expert, n_used, wg, wu, wd, *, tm, tf=512):
    m, d = xs.shape
    f = wg.shape[2]
    nf = f // tf

    def fcol(i, j, nu):
        return jnp.where(i < nu[0], j, nf - 1)

    grid_spec = pltpu.PrefetchScalarGridSpec(
        num_scalar_prefetch=2,
        grid=(m // tm, nf),
        in_specs=[pl.BlockSpec((tm, d), lambda i, j, te, nu: (i, 0)),
                  pl.BlockSpec((None, d, tf), lambda i, j, te, nu: (te[i], 0, fcol(i, j, nu))),
                  pl.BlockSpec((None, d, tf), lambda i, j, te, nu: (te[i], 0, fcol(i, j, nu))),
                  pl.BlockSpec((None, tf, d), lambda i, j, te, nu: (te[i], fcol(i, j, nu), 0))],
        out_specs=pl.BlockSpec((tm, d), lambda i, j, te, nu: (i, 0)),
        scratch_shapes=[pltpu.VMEM((tm, d), MXU_DTYPE)])
    return pl.pallas_call(
        _grouped_ffn_kernel,
        grid_spec=grid_spec,
        out_shape=jax.ShapeDtypeStruct((m, d), F32),
        compiler_params=_cparams(("arbitrary", "arbitrary")),
        name="grouped_ffn",
    )(tile_expert, n_used, xs, wg, wu, wd)


META_E, META_W, META_RANK = 0, 2, 4


def _router_kernel(x_ref, g_ref, wr_ref, h_ref, meta_ref, cnt_ref, carry_s, *, tm):
    @pl.when(pl.program_id(0) == 0)
    def _():
        carry_s[...] = jnp.zeros_like(carry_s)

    h = _rms(x_ref[...], g_ref[...])
    h_ref[...] = h
    logits = jnp.dot(h, wr_ref[...], precision=HIGHEST, preferred_element_type=F32)
    lane = lax.broadcasted_iota(jnp.int32, (tm, LANES), 1).astype(F32)
    lg = jnp.where(lane < N_EXPERTS, logits, -jnp.inf)

    def take_top(v):
        top = jnp.max(v, axis=-1, keepdims=True)
        idx = jnp.min(jnp.where(v == top, lane, float(LANES)), axis=-1, keepdims=True)
        return top, idx

    m1, i1 = take_top(lg)
    m2, i2 = take_top(jnp.where(lane == i1, -jnp.inf, lg))
    e = jnp.exp(m2 - m1)
    w1 = 1.0 / (1.0 + e)
    w2 = e / (1.0 + e)
    oh1 = lane == i1
    oh2 = lane == i2
    oh = jnp.where(oh1 | oh2, 1.0, 0.0)
    r = lax.broadcasted_iota(jnp.int32, (tm, tm), 0)
    c = lax.broadcasted_iota(jnp.int32, (tm, tm), 1)
    lower = jnp.where(c < r, 1.0, 0.0).astype(BF16)
    before = _dot(lower, oh.astype(BF16)) + carry_s[...]
    rank1 = jnp.sum(jnp.where(oh1, before, 0.0), axis=-1, keepdims=True)
    rank2 = jnp.sum(jnp.where(oh2, before, 0.0), axis=-1, keepdims=True)
    carry_s[...] += jnp.sum(oh, axis=0, keepdims=True)
    meta = jnp.zeros((tm, LANES), F32)
    for k, v in ((META_E, i1), (META_E + 1, i2), (META_W, w1), (META_W + 1, w2),
                 (META_RANK, rank1), (META_RANK + 1, rank2)):
        meta = jnp.where(lane == k, v, meta)
    meta_ref[...] = meta
    cnt_ref[...] = jnp.broadcast_to(carry_s[...], cnt_ref.shape)


def moe_router(x, g, w_router, *, tm=512):
    t, d = x.shape
    tm = min(tm, t)
    wr = jnp.pad(w_router, ((0, 0), (0, LANES - N_EXPERTS)))
    return pl.pallas_call(
        functools.partial(_router_kernel, tm=tm),
        grid=(t // tm,),
        in_specs=[pl.BlockSpec((tm, d), lambda i: (i, 0)),
                  pl.BlockSpec((1, d), lambda i: (0, 0)),
                  pl.BlockSpec((d, LANES), lambda i: (0, 0))],
        out_specs=[pl.BlockSpec((tm, d), lambda i: (i, 0)),
                   pl.BlockSpec((tm, LANES), lambda i: (i, 0)),
                   pl.BlockSpec((8, LANES), lambda i: (0, 0))],
        out_shape=[jax.ShapeDtypeStruct((t, d), F32),
                   jax.ShapeDtypeStruct((t, LANES), F32),
                   jax.ShapeDtypeStruct((8, LANES), F32)],
        scratch_shapes=[pltpu.VMEM((1, LANES), F32)],
        compiler_params=_cparams(("arbitrary",)),
        name="moe_router",
    )(x, g.reshape(1, d), wr)


def _dispatch_kernel(p1_ref, p2_ref, h_ref, init_ref, xs_ref, sem, *, tr):
    del init_ref
    base = pl.program_id(0) * tr

    def copies(r, tok):
        src = h_ref.at[pl.ds(r, 1)]
        return (pltpu.make_async_copy(src, xs_ref.at[pl.ds(p1_ref[tok], 1)], sem.at[0]),
                pltpu.make_async_copy(src, xs_ref.at[pl.ds(p2_ref[tok], 1)], sem.at[1]))

    def issue(r, carry):
        for c in copies(r, base + r):
            c.start()
        return carry

    lax.fori_loop(0, tr, issue, 0)

    def drain(r, carry):
        for c in copies(0, 0):
            c.wait()
        return carry

    lax.fori_loop(0, tr, drain, 0)


def moe_dispatch(h, pos1, pos2, m_pad, *, tr=256):
    t, d = h.shape
    tr = min(tr, t)
    grid_spec = pltpu.PrefetchScalarGridSpec(
        num_scalar_prefetch=2,
        grid=(t // tr,),
        in_specs=[pl.BlockSpec((tr, d), lambda i, p1, p2: (i, 0)),
                  pl.BlockSpec(memory_space=pl.ANY)],
        out_specs=pl.BlockSpec(memory_space=pl.ANY),
        scratch_shapes=[pltpu.SemaphoreType.DMA((2,))])
    return pl.pallas_call(
        functools.partial(_dispatch_kernel, tr=tr),
        grid_spec=grid_spec,
        out_shape=jax.ShapeDtypeStruct((m_pad, d), h.dtype),
        input_output_aliases={3: 0},
        compiler_params=_cparams(("arbitrary",)),
        name="moe_dispatch",
    )(pos1, pos2, h, jnp.zeros((m_pad, d), h.dtype))


def _combine_kernel(p1_ref, p2_ref, x_ref, meta_ref, y_ref, o_ref, y1_s, y2_s, sem, *, tr):
    base = pl.program_id(0) * tr

    def copies(r, tok):
        return (pltpu.make_async_copy(y_ref.at[pl.ds(p1_ref[tok], 1)], y1_s.at[pl.ds(r, 1)], sem.at[0]),
                pltpu.make_async_copy(y_ref.at[pl.ds(p2_ref[tok], 1)], y2_s.at[pl.ds(r, 1)], sem.at[1]))

    def issue(r, carry):
        for c in copies(r, base + r):
            c.start()
        return carry

    lax.fori_loop(0, tr, issue, 0)

    def drain(r, carry):
        for c in copies(0, 0):
            c.wait()
        return carry

    lax.fori_loop(0, tr, drain, 0)
    meta = meta_ref[...]
    w1 = meta[:, META_W:META_W + 1]
    w2 = meta[:, META_W + 1:META_W + 2]
    o_ref[...] = x_ref[...] + w1 * y1_s[...] + w2 * y2_s[...]


def moe_combine(x, meta, y, pos1, pos2, *, tr=256):
    t, d = x.shape
    tr = min(tr, t)
    grid_spec = pltpu.PrefetchScalarGridSpec(
        num_scalar_prefetch=2,
        grid=(t // tr,),
        in_specs=[pl.BlockSpec((tr, d), lambda i, p1, p2: (i, 0)),
                  pl.BlockSpec((tr, LANES), lambda i, p1, p2: (i, 0)),
                  pl.BlockSpec(memory_space=pl.ANY)],
        out_specs=pl.BlockSpec((tr, d), lambda i, p1, p2: (i, 0)),
        scratch_shapes=[pltpu.VMEM((tr, d), F32), pltpu.VMEM((tr, d), F32),
                        pltpu.SemaphoreType.DMA((2,))])
    return pl.pallas_call(
        functools.partial(_combine_kernel, tr=tr),
        grid_spec=grid_spec,
        out_shape=jax.ShapeDtypeStruct((t, d), F32),
        compiler_params=_cparams(("arbitrary",)),
        name="moe_combine",
    )(pos1, pos2, x, meta, y)


def moe_ffn(x, ffn_norm, w_router, wg, wu, wd, *, tm=512):
    t, d = x.shape
    tm = min(tm, t)
    h, meta, cnt = moe_router(x, ffn_norm, w_router)
    counts = cnt[0, :N_EXPERTS].astype(jnp.int32)
    padded = (counts + tm - 1) // tm * tm
    ends = jnp.cumsum(padded)
    starts = ends - padded
    ids = meta[:, META_E:META_E + 2].astype(jnp.int32)
    ranks = meta[:, META_RANK:META_RANK + 2].astype(jnp.int32)
    pos = starts[ids] + ranks
    pos1, pos2 = pos[:, 0], pos[:, 1]
    m_pad = TOP_K * t + N_EXPERTS * tm
    n_tiles = m_pad // tm
    tile_start = jnp.arange(n_tiles, dtype=jnp.int32) * tm
    n_used = (ends[-1] // tm).astype(jnp.int32)
    tile_expert = jnp.sum((tile_start[:, None] >= ends[None, :]).astype(jnp.int32), axis=1)
    last_expert = jnp.sum((ends[-1] - 1 >= ends).astype(jnp.int32))
    tile_expert = jnp.where(tile_start < ends[-1], tile_expert, last_expert).astype(jnp.int32)
    xs = moe_dispatch(h, pos1, pos2, m_pad)
    y = grouped_ffn(xs, tile_expert, n_used.reshape(1), wg, wu, wd, tm=tm)
    return moe_combine(x, meta, y, pos1, pos2)


def _compress_kernel(c_ref, pe_ref, w1_ref, w2_ref, ng_ref, o_ref, *, nrows, half, normalize):
    c = c_ref[...].astype(MXU_DTYPE)
    top = _dot(c, w1_ref[0:half, :])
    bot = _dot(c, w1_ref[half:, :])
    bot_next = pltpu.roll(bot, nrows - 1, 0)
    pe_term = _dot(pe_ref[...].astype(MXU_DTYPE), w1_ref[...])
    hid = top + bot_next + pe_term
    out = _dot((hid * _sigmoid(hid)).astype(MXU_DTYPE), w2_ref[...])
    if normalize:
        out = _rms(out, ng_ref[...])
    valid = lax.broadcasted_iota(jnp.int32, (nrows, 1), 0) < nrows - 1
    o_ref[...] = jnp.where(valid, out, 0.0)


def compress(c, pe, w1, w2, norm_g, *, normalize):
    g, nrows, half = c.shape
    return pl.pallas_call(
        functools.partial(_compress_kernel, nrows=nrows, half=half, normalize=normalize),
        grid=(g,),
        in_specs=[pl.BlockSpec((None, nrows, half), lambda i: (i, 0, 0)),
                  pl.BlockSpec((1, 2 * half), lambda i: (0, 0)),
                  pl.BlockSpec((2 * half, CMP_HIDDEN), lambda i: (0, 0)),
                  pl.BlockSpec((CMP_HIDDEN, HEAD_DIM), lambda i: (0, 0)),
                  pl.BlockSpec((1, HEAD_DIM), lambda i: (0, 0))],
        out_specs=pl.BlockSpec((None, nrows, HEAD_DIM), lambda i: (i, 0, 0)),
        out_shape=jax.ShapeDtypeStruct((g, nrows, HEAD_DIM), F32),
        compiler_params=_cparams(("arbitrary",)),
        name="nsa_compress",
    )(c, pe.reshape(1, 2 * half), w1, w2, norm_g.reshape(1, HEAD_DIM))


def _nsa_cmp_kernel(slopes_ref, q_ref, kc_ref, vc_ref, qg_ref, ov_ref, oc_ref, sel_ref, used_ref, kc_s, vct_s,
                    psum_s, *, tq, ncmp, nsel):
    g = pl.program_id(0)
    qi = pl.program_id(1)

    @pl.when(qi == 0)
    def _():
        kc_s[...] = kc_ref[...].astype(kc_s.dtype)
        vct_s[...] = vc_ref[...].T.astype(vct_s.dtype)

    def heads(nk):
        tpos = qi * tq + lax.broadcasted_iota(jnp.int32, (nk, tq), 1)
        cend = lax.broadcasted_iota(jnp.int32, (nk, tq), 0) * CMP_STRIDE + (CMP_LEN - 1)
        dist = tpos - cend
        ok = dist >= 0
        dist_f = dist.astype(F32)
        any_ok = jnp.where(tpos[0:1, :] >= CMP_LEN - 1, 1.0, 0.0)
        psum = jnp.zeros((nk, tq), F32)
        for r in range(NSA_GROUP):
            cols = slice(r * HEAD_DIM, (r + 1) * HEAD_DIM)
            qn = _rms(q_ref[:, cols], qg_ref[...])
            s = _dot_nt(kc_s[0:nk, :], (qn * HEAD_DIM ** -0.5).astype(MXU_DTYPE))
            s = jnp.where(ok, s - slopes_ref[g * NSA_GROUP + r] * dist_f, NEG)
            p = jnp.exp(s - jnp.max(s, axis=0, keepdims=True))
            p = p * (any_ok / jnp.sum(p, axis=0, keepdims=True))
            oc_ref[:, cols] = _dot(vct_s[:, 0:nk], p.astype(MXU_DTYPE)).T.astype(oc_ref.dtype)
            psum = psum + p
        psum_s[0:nk, :] = psum
        if nk < ncmp:
            psum_s[nk:, :] = jnp.zeros((ncmp - nk, tq), F32)

    n_prefix = ncmp // LANES if ncmp % LANES == 0 else 1
    step_keys = ncmp // n_prefix
    n_visible = (qi * tq + tq - CMP_LEN) // CMP_STRIDE + 1
    need = jnp.minimum((n_visible + step_keys - 1) // step_keys, n_prefix)
    for v in range(1, n_prefix + 1):
        @pl.when(need == v)
        def _():
            heads(v * step_keys)

    psum = psum_s[...]
    p_hi = psum.astype(MXU_DTYPE)
    p_lo = (psum - p_hi.astype(F32)).astype(MXU_DTYPE)
    imp = _dot(ov_ref[...], p_hi) + _dot(ov_ref[...], p_lo)
    blk = lax.broadcasted_iota(jnp.int32, (LANES, tq), 0)
    qb = (qi * tq + lax.broadcasted_iota(jnp.int32, (LANES, tq), 1)) // SLC_BLOCK
    forced = (blk == 0) | (blk == qb) | (blk == qb - 1)
    visible = blk <= qb
    imp = jnp.where(visible & jnp.logical_not(forced), imp, -jnp.inf)
    blk_f = blk.astype(F32)
    sel = forced
    for _ in range(min(SLC_TOPN, nsel) - 3):
        top = jnp.max(imp, axis=0, keepdims=True)
        first = jnp.min(jnp.where(imp == top, blk_f, float(LANES)), axis=0, keepdims=True)
        pick = blk_f == first
        sel = sel | pick
        imp = jnp.where(pick, -jnp.inf, imp)
    sel_q = jnp.where(sel & visible, 1.0, 0.0).T
    sel_ref[...] = sel_q.astype(sel_ref.dtype)
    used_ref[...] = jnp.broadcast_to(jnp.max(sel_q, axis=0, keepdims=True), used_ref.shape)


def nsa_compressed(u, k_cmp, v_cmp, q_norm, *, tq=256):
    t = u.shape[0]
    tq = min(tq, t)
    ncmp = k_cmp.shape[1]
    nsel = t // SLC_BLOCK
    assert nsel <= LANES
    nstart = np.arange(ncmp)[:, None] * CMP_STRIDE
    lo = np.arange(LANES)[None, :] * SLC_BLOCK
    overlap = ((nstart < lo + SLC_BLOCK) & (nstart + CMP_LEN > lo) & (np.arange(ncmp)[:, None] < ncmp - 1))
    ov_t = jnp.asarray(overlap.T.astype(np.float32)).astype(MXU_DTYPE)
    gw = NSA_GROUP * HEAD_DIM
    grid_spec = pltpu.PrefetchScalarGridSpec(
        num_scalar_prefetch=1,
        grid=(NSA_KV_HEADS, t // tq),
        in_specs=[pl.BlockSpec((tq, gw), lambda g, i, s: (i, g)),
                  pl.BlockSpec((None, ncmp, HEAD_DIM), lambda g, i, s: (g, 0, 0)),
                  pl.BlockSpec((None, ncmp, HEAD_DIM), lambda g, i, s: (g, 0, 0)),
                  pl.BlockSpec((1, HEAD_DIM), lambda g, i, s: (0, 0)),
                  pl.BlockSpec((LANES, ncmp), lambda g, i, s: (0, 0))],
        out_specs=[pl.BlockSpec((tq, gw), lambda g, i, s: (i, g)),
                   pl.BlockSpec((None, tq, LANES), lambda g, i, s: (g, i, 0)),
                   pl.BlockSpec((None, None, 8, LANES), lambda g, i, s: (g, i, 0, 0))],
        scratch_shapes=[pltpu.VMEM((ncmp, HEAD_DIM), MXU_DTYPE), pltpu.VMEM((HEAD_DIM, ncmp), MXU_DTYPE),
                        pltpu.VMEM((ncmp, tq), F32)])
    return pl.pallas_call(
        functools.partial(_nsa_cmp_kernel, tq=tq, ncmp=ncmp, nsel=nsel),
        grid_spec=grid_spec,
        out_shape=[jax.ShapeDtypeStruct((t, NSA_DIM), F32),
                   jax.ShapeDtypeStruct((NSA_KV_HEADS, t, LANES), MXU_DTYPE),
                   jax.ShapeDtypeStruct((NSA_KV_HEADS, t // tq, 8, LANES), F32)],
        compiler_params=_cparams(("arbitrary", "arbitrary")),
        name="nsa_compressed",
    )(_nsa_slopes(), u, k_cmp, v_cmp, q_norm.reshape(1, HEAD_DIM), ov_t)


def _nsa_slopes():
    return jnp.asarray(2.0 ** (-8.0 * np.arange(1, NSA_HEADS + 1) / NSA_HEADS), dtype=F32)


def _group_queries(q_ref, qg_ref, qaug_s, tq):
    for r in range(NSA_GROUP):
        qn = _rms(q_ref[:, r * HEAD_DIM:(r + 1) * HEAD_DIM], qg_ref[...])
        qaug_s[r * tq:(r + 1) * tq, 0:HEAD_DIM] = (qn * QK_SCALE_LOG2).astype(qaug_s.dtype)


def _nsa_slc_kernel(slopes_ref, tiles_ref, npast_ref, q_ref, k_ref, v_ref, sel_ref, qg_ref, kg_ref, o_ref,
                    kaug_s, vt_s, qaug_s, sb_s, s_s, m_s, acc_s, *, tq, t_total):
    g = pl.program_id(0)
    qi = pl.program_id(1)
    tk = tq
    row_len = t_total // tk + 1
    slopes = [slopes_ref[g * NSA_GROUP + r] for r in range(NSA_GROUP)]

    @pl.when(qi == 0)
    def _():
        _kv_prologue(k_ref, v_ref, kg_ref, kaug_s, vt_s, t=t_total, tk=tk, blk=SLC_BLOCK)
        _fill_bias(sb_s, slopes, (PLAIN, CAUSAL), tk=tk, tq=tq)

    _group_queries(q_ref, qg_ref, qaug_s, tq)
    sel_bias = jnp.where(sel_ref[...].astype(F32) > 0.5, 0.0, NEG).astype(qaug_s.dtype)
    for r in range(NSA_GROUP):
        qaug_s[r * tq:(r + 1) * tq, HEAD_DIM:] = sel_bias
    _flash_init(m_s, acc_s)
    qk, spv = _flash_stages(t0=qi * tq, slopes=slopes, qaug_s=qaug_s, kaug_s=kaug_s, vt_s=vt_s, sb_s=sb_s,
                            s_s=s_s, m_s=m_s, acc_s=acc_s, tq=tq, tk=tk)
    row = g * pl.num_programs(1) + qi
    _causal_sweep(qk, spv, lambda k: tiles_ref[row * row_len + k], npast_ref[row])
    for r in range(NSA_GROUP):
        o_ref[:, r * HEAD_DIM:(r + 1) * HEAD_DIM] = _flash_result(acc_s, r, tq).astype(o_ref.dtype)


def _nsa_win_kernel(slopes_ref, q_ref, k_ref, v_ref, oc_ref, os_ref, gl_ref, qg_ref, kg_ref, o_ref,
                    kaug_s, vt_s, qaug_s, sb_s, s_s, m_s, acc_s, *, tq, t_total):
    g = pl.program_id(0)
    qi = pl.program_id(1)
    tk = tq
    nw = WINDOW // tk
    slopes = [slopes_ref[g * NSA_GROUP + r] for r in range(NSA_GROUP)]

    @pl.when(qi == 0)
    def _():
        _kv_prologue(k_ref, v_ref, kg_ref, kaug_s, vt_s, t=t_total, tk=tk, blk=None)
        _fill_bias(sb_s, slopes, (PLAIN, CAUSAL, WINDOW_TAIL), tk=tk, tq=tq)

    _group_queries(q_ref, qg_ref, qaug_s, tq)
    _flash_init(m_s, acc_s)
    qk, spv = _flash_stages(t0=qi * tq, slopes=slopes, qaug_s=qaug_s, kaug_s=kaug_s, vt_s=vt_s, sb_s=sb_s,
                            s_s=s_s, m_s=m_s, acc_s=acc_s, tq=tq, tk=tk)
    kinds = [WINDOW_TAIL] + [PLAIN] * (nw - 1) + [CAUSAL]

    @pl.when(qi >= nw)
    def _():
        for i in range(nw + 1):
            qk(qi - nw + i, i)
        for i, kind in enumerate(kinds):
            spv(qi - nw + i, i, kind)

    @pl.when(qi < nw)
    def _():
        for back in range(nw - 1, -1, -1):
            @pl.when(qi >= back)
            def _():
                qk(qi - back, 0)
                spv(qi - back, 0, kinds[nw - back])
    gates = _sigmoid(gl_ref[...])
    for r in range(NSA_GROUP):
        cols = slice(r * HEAD_DIM, (r + 1) * HEAD_DIM)
        lane0 = 3 * r
        o = (gates[:, lane0:lane0 + 1] * oc_ref[:, cols] + gates[:, lane0 + 1:lane0 + 2] * os_ref[:, cols]
             + gates[:, lane0 + 2:lane0 + 3] * _flash_result(acc_s, r, tq))
        o_ref[:, cols] = o.astype(o_ref.dtype)


def nsa_selected(u, sel, blocks_used, q_norm, k_norm, *, kcol, vcol, tq=256):
    t = u.shape[0]
    tq = min(tq, t)
    gw = NSA_GROUP * HEAD_DIM
    per_tile = tq // SLC_BLOCK
    nt = t // tq
    used = blocks_used[:, :, 0, :t // SLC_BLOCK].reshape(NSA_KV_HEADS, nt, nt, per_tile)
    tile_ids = jnp.arange(nt, dtype=jnp.int32)
    past_used = (jnp.max(used, axis=-1) > 0) & (tile_ids[None, None, :] < tile_ids[None, :, None])
    n_past = jnp.sum(past_used, axis=-1).astype(jnp.int32)
    order = jnp.argsort(jnp.logical_not(past_used), axis=-1, stable=True).astype(jnp.int32)
    slots = jnp.arange(nt + 1, dtype=jnp.int32)
    order = jnp.concatenate([order, order[..., :1]], axis=-1)
    tiles = jnp.where(slots < n_past[..., None], order, tile_ids[None, :, None])
    grid_spec = pltpu.PrefetchScalarGridSpec(
        num_scalar_prefetch=3,
        grid=(NSA_KV_HEADS, nt),
        in_specs=[pl.BlockSpec((tq, gw), lambda g, i, s, tl, n: (i, g)),
                  pl.BlockSpec((t, HEAD_DIM), lambda g, i, s, tl, n: (0, kcol + g)),
                  pl.BlockSpec((t, HEAD_DIM), lambda g, i, s, tl, n: (0, vcol + g)),
                  pl.BlockSpec((None, tq, LANES), lambda g, i, s, tl, n: (g, i, 0)),
                  pl.BlockSpec((1, HEAD_DIM), lambda g, i, s, tl, n: (0, 0)),
                  pl.BlockSpec((1, HEAD_DIM), lambda g, i, s, tl, n: (0, 0))],
        out_specs=pl.BlockSpec((tq, gw), lambda g, i, s, tl, n: (i, g)),
        scratch_shapes=_flash_scratch(t, tq, tq, NSA_GROUP, HEAD_DIM + LANES, 2, 2))
    return pl.pallas_call(
        functools.partial(_nsa_slc_kernel, tq=tq, t_total=t),
        grid_spec=grid_spec,
        out_shape=jax.ShapeDtypeStruct((t, NSA_DIM), F32),
        compiler_params=_cparams(("arbitrary", "arbitrary")),
        name="nsa_selected",
    )(_nsa_slopes(), tiles.reshape(-1), n_past.reshape(-1), u, u, u, sel,
      q_norm.reshape(1, HEAD_DIM), k_norm.reshape(1, HEAD_DIM))


def nsa_window_merge(u, o_cmp, o_slc, gate_logits, q_norm, k_norm, *, kcol, vcol, tq=256):
    t = u.shape[0]
    tq = min(tq, t)
    assert WINDOW % tq == 0
    gw = NSA_GROUP * HEAD_DIM
    gl = gate_logits.reshape(t, NSA_KV_HEADS, NSA_GROUP * 3)
    gl = jnp.pad(gl, ((0, 0), (0, 0), (0, LANES - NSA_GROUP * 3))).reshape(t, NSA_KV_HEADS * LANES)
    grid_spec = pltpu.PrefetchScalarGridSpec(
        num_scalar_prefetch=1,
        grid=(NSA_KV_HEADS, t // tq),
        in_specs=[pl.BlockSpec((tq, gw), lambda g, i, s: (i, g)),
                  pl.BlockSpec((t, HEAD_DIM), lambda g, i, s: (0, kcol + g)),
                  pl.BlockSpec((t, HEAD_DIM), lambda g, i, s: (0, vcol + g)),
                  pl.BlockSpec((tq, gw), lambda g, i, s: (i, g)),
                  pl.BlockSpec((tq, gw), lambda g, i, s: (i, g)),
                  pl.BlockSpec((tq, LANES), lambda g, i, s: (i, g)),
                  pl.BlockSpec((1, HEAD_DIM), lambda g, i, s: (0, 0)),
                  pl.BlockSpec((1, HEAD_DIM), lambda g, i, s: (0, 0))],
        out_specs=pl.BlockSpec((tq, gw), lambda g, i, s: (i, g)),
        scratch_shapes=_flash_scratch(t, tq, tq, NSA_GROUP, HEAD_DIM, 3, WINDOW // tq + 1))
    return pl.pallas_call(
        functools.partial(_nsa_win_kernel, tq=tq, t_total=t),
        grid_spec=grid_spec,
        out_shape=jax.ShapeDtypeStruct((t, NSA_DIM), MXU_DTYPE),
        compiler_params=_cparams(("arbitrary", "arbitrary")),
        name="nsa_window_merge",
    )(_nsa_slopes(), u, u, u, o_cmp, o_slc, gl, q_norm.reshape(1, HEAD_DIM), k_norm.reshape(1, HEAD_DIM))


def _mx(w):
    return w.astype(MXU_DTYPE)


def conv_moba_layer(x, attn_norm, w_in, conv_w, conv_b, conv_norm_g, conv_norm_b, q_norm, k_norm,
                    w_out, ffn_norm, w_gate, w_up, w_down):
    u = norm_matmul(x, attn_norm, _mx(w_in))
    a = conv_module(u, conv_w, conv_b, conv_norm_g, conv_norm_b)
    o = moba_attention(u, q_norm, k_norm, col0=2 * CONV_CH // HEAD_DIM)
    w_out = _mx(w_out)
    x = proj_residual(x, [(a, w_out[:CONV_CH]), (o, w_out[CONV_CH:])])
    return dense_ffn(x, ffn_norm, _mx(w_gate), _mx(w_up), _mx(w_down))


def nsa_moe_layer(x, attn_norm, w_in, q_norm, kc_norm, ks_norm, kw_norm, cmp_pe_k, cmp_w1_k, cmp_w2_k,
                  cmp_pe_v, cmp_w1_v, cmp_w2_v, w_out, ffn_norm, w_router, w_gate, w_up, w_down):
    t = x.shape[0]
    main = NSA_DIM + 6 * NSA_KV_DIM
    n_gate = 3 * NSA_HEADS
    w_in_p = jnp.pad(_mx(w_in), ((0, 0), (0, 1024 - n_gate)))
    u = norm_matmul(x, attn_norm, w_in_p)
    gate_logits = u[:, main:main + n_gate]
    nrows = t // CMP_STRIDE

    def blocks(col):
        v = u[:, col:col + NSA_KV_DIM].reshape(nrows, CMP_STRIDE, NSA_KV_HEADS, HEAD_DIM)
        return v.transpose(2, 0, 1, 3).reshape(NSA_KV_HEADS, nrows, CMP_STRIDE * HEAD_DIM)

    k_cmp = compress(blocks(NSA_DIM), cmp_pe_k, _mx(cmp_w1_k), _mx(cmp_w2_k), kc_norm, normalize=True)
    v_cmp = compress(blocks(NSA_DIM + NSA_KV_DIM), cmp_pe_v, _mx(cmp_w1_v), _mx(cmp_w2_v), kc_norm,
                     normalize=False)
    o_cmp, sel, blocks_used = nsa_compressed(u, k_cmp, v_cmp, q_norm)
    cb = lambda col: col // HEAD_DIM
    o_slc = nsa_selected(u, sel, blocks_used, q_norm, ks_norm, kcol=cb(NSA_DIM + 2 * NSA_KV_DIM),
                         vcol=cb(NSA_DIM + 3 * NSA_KV_DIM))
    o = nsa_window_merge(u, o_cmp, o_slc, gate_logits, q_norm, kw_norm,
                         kcol=cb(NSA_DIM + 4 * NSA_KV_DIM), vcol=cb(NSA_DIM + 5 * NSA_KV_DIM))
    x = proj_residual(x, [(o, _mx(w_out))])
    return moe_ffn(x, ffn_norm, w_router, _mx(w_gate), _mx(w_up), _mx(w_down))


def kernel(x, l0_attn_norm, l0_w_in, l0_conv_w, l0_conv_b, l0_conv_norm_g, l0_conv_norm_b, l0_q_norm, l0_k_norm, l0_w_out, l0_ffn_norm, l0_w_gate, l0_w_up, l0_w_down, l1_attn_norm, l1_w_in, l1_q_norm, l1_kc_norm, l1_ks_norm, l1_kw_norm, l1_cmp_pe_k, l1_cmp_w1_k, l1_cmp_w2_k, l1_cmp_pe_v, l1_cmp_w1_v, l1_cmp_w2_v, l1_w_out, l1_ffn_norm, l1_w_router, l1_w_gate, l1_w_up, l1_w_down, l2_attn_norm, l2_w_in, l2_conv_w, l2_conv_b, l2_conv_norm_g, l2_conv_norm_b, l2_q_norm, l2_k_norm, l2_w_out, l2_ffn_norm, l2_w_gate, l2_w_up, l2_w_down, l3_attn_norm, l3_w_in, l3_q_norm, l3_kc_norm, l3_ks_norm, l3_kw_norm, l3_cmp_pe_k, l3_cmp_w1_k, l3_cmp_w2_k, l3_cmp_pe_v, l3_cmp_w1_v, l3_cmp_w2_v, l3_w_out, l3_ffn_norm, l3_w_router, l3_w_gate, l3_w_up, l3_w_down):
    b, t, d = x.shape
    assert b == 1 and d == D_MODEL
    h = x.reshape(t, d)
    h = conv_moba_layer(h, l0_attn_norm, l0_w_in, l0_conv_w, l0_conv_b, l0_conv_norm_g, l0_conv_norm_b, l0_q_norm, l0_k_norm, l0_w_out, l0_ffn_norm, l0_w_gate, l0_w_up, l0_w_down)
    h = nsa_moe_layer(h, l1_attn_norm, l1_w_in, l1_q_norm, l1_kc_norm, l1_ks_norm, l1_kw_norm, l1_cmp_pe_k, l1_cmp_w1_k, l1_cmp_w2_k, l1_cmp_pe_v, l1_cmp_w1_v, l1_cmp_w2_v, l1_w_out, l1_ffn_norm, l1_w_router, l1_w_gate, l1_w_up, l1_w_down)
    h = conv_moba_layer(h, l2_attn_norm, l2_w_in, l2_conv_w, l2_conv_b, l2_conv_norm_g, l2_conv_norm_b, l2_q_norm, l2_k_norm, l2_w_out, l2_ffn_norm, l2_w_gate, l2_w_up, l2_w_down)
    h = nsa_moe_layer(h, l3_attn_norm, l3_w_in, l3_q_norm, l3_kc_norm, l3_ks_norm, l3_kw_norm, l3_cmp_pe_k, l3_cmp_w1_k, l3_cmp_w2_k, l3_cmp_pe_v, l3_cmp_w1_v, l3_cmp_w2_v, l3_w_out, l3_ffn_norm, l3_w_router, l3_w_gate, l3_w_up, l3_w_down)
    return h.reshape(b, t, d)
```

```python
import functools

import numpy as np
import jax
import jax.numpy as jnp
from jax import lax
from jax.experimental import pallas as pl
from jax.experimental.pallas import tpu as pltpu

D_MODEL = 2048
HEAD_DIM = 128
CONV_CH = D_MODEL // 2
CONV_WIDTH = 31
MOBA_HEADS = (D_MODEL // 2) // HEAD_DIM
MOBA_DIM = MOBA_HEADS * HEAD_DIM
MOBA_BLOCK = 256
MOBA_TOPK = 3
NSA_HEADS = D_MODEL // HEAD_DIM
NSA_KV_HEADS = NSA_HEADS // 4
NSA_GROUP = NSA_HEADS // NSA_KV_HEADS
NSA_DIM = NSA_HEADS * HEAD_DIM
NSA_KV_DIM = NSA_KV_HEADS * HEAD_DIM
CMP_LEN = 32
CMP_STRIDE = 16
CMP_HIDDEN = 256
SLC_BLOCK = 64
SLC_TOPN = 16
WINDOW = 512
D_FF = ((8 * D_MODEL) // 3 + 255) // 256 * 256
N_EXPERTS = 8
TOP_K = 2
EPS = 1e-6
NEG = -1e30

LANES = 128
F32 = jnp.float32
BF16 = jnp.bfloat16
MXU_DTYPE = BF16
HIGHEST = lax.Precision.HIGHEST
VMEM_LIMIT = 56 * 1024 * 1024


def _cparams(sem):
    return pltpu.CompilerParams(dimension_semantics=sem, vmem_limit_bytes=VMEM_LIMIT)


def _rms(x, g):
    return x * lax.rsqrt(jnp.mean(x * x, axis=-1, keepdims=True) + EPS) * g


def _dot(a, b):
    return jnp.dot(a, b, preferred_element_type=F32)


def _dot_nt(a, b, precision=None):
    return lax.dot_general(a, b, (((1,), (1,)), ((), ())), precision=precision,
                           preferred_element_type=F32)


def _sigmoid(x):
    return 1.0 / (1.0 + jnp.exp(-x))


def _norm_matmul_kernel(x_ref, g_ref, w_ref, o_ref, h_ref):
    @pl.when(pl.program_id(1) == 0)
    def _():
        h_ref[...] = _rms(x_ref[...], g_ref[...]).astype(h_ref.dtype)

    o_ref[...] = _dot(h_ref[...], w_ref[...]).astype(o_ref.dtype)


def norm_matmul(x, g, w, *, tm=1024, tn=1024):
    t, d = x.shape
    n = w.shape[1]
    tm = min(tm, t)
    assert t % tm == 0 and n % tn == 0
    return pl.pallas_call(
        _norm_matmul_kernel,
        grid=(t // tm, n // tn),
        in_specs=[pl.BlockSpec((tm, d), lambda i, j: (i, 0)),
                  pl.BlockSpec((1, d), lambda i, j: (0, 0)),
                  pl.BlockSpec((d, tn), lambda i, j: (0, j))],
        out_specs=pl.BlockSpec((tm, tn), lambda i, j: (i, j)),
        out_shape=jax.ShapeDtypeStruct((t, n), F32),
        scratch_shapes=[pltpu.VMEM((tm, d), MXU_DTYPE)],
        compiler_params=_cparams(("arbitrary", "arbitrary")),
        name="norm_matmul",
    )(x, g.reshape(1, d), w)


def _proj_residual_kernel(*refs, n_pairs):
    x_ref = refs[0]
    o_ref = refs[1 + 2 * n_pairs]
    acc = x_ref[...]
    for p in range(n_pairs):
        acc = acc + _dot(refs[1 + 2 * p][...], refs[2 + 2 * p][...])
    o_ref[...] = acc


def proj_residual(x, pairs, *, tm=1024, tn=1024):
    t, d = x.shape
    tm = min(tm, t)
    in_specs = [pl.BlockSpec((tm, tn), lambda i, j: (i, j))]
    args = [x]
    for a, w in pairs:
        k = a.shape[1]
        in_specs.append(pl.BlockSpec((tm, k), lambda i, j: (i, 0)))
        in_specs.append(pl.BlockSpec((k, tn), lambda i, j: (0, j)))
        args += [a, w]
    return pl.pallas_call(
        functools.partial(_proj_residual_kernel, n_pairs=len(pairs)),
        grid=(t // tm, d // tn),
        in_specs=in_specs,
        out_specs=pl.BlockSpec((tm, tn), lambda i, j: (i, j)),
        out_shape=jax.ShapeDtypeStruct((t, d), F32),
        compiler_params=_cparams(("arbitrary", "arbitrary")),
        name="proj_residual",
    )(*args)


CONV_HALO = 32
CONV_ROWS = 32


def _conv_kernel(av_ref, ag_ref, hv_ref, hg_ref, w_ref, b_ref, lg_ref, lb_ref, o_ref, s_ref, *, tq):
    i = pl.program_id(0)
    halo = hv_ref[...] * _sigmoid(hg_ref[...])
    s_ref[0:CONV_HALO, :] = jnp.where(i > 0, halo, 0.0)
    s_ref[CONV_HALO:, :] = av_ref[...] * _sigmoid(ag_ref[...])
    first = CONV_HALO - (CONV_WIDTH - 1)
    for c in range(tq // CONV_ROWS):
        base = c * CONV_ROWS + first
        acc = jnp.broadcast_to(b_ref[...], (CONV_ROWS, CONV_CH))
        for k in range(CONV_WIDTH):
            acc = acc + s_ref[base + k:base + k + CONV_ROWS, :] * w_ref[k:k + 1, :]
        mu = jnp.mean(acc, axis=-1, keepdims=True)
        cen = acc - mu
        var = jnp.mean(cen * cen, axis=-1, keepdims=True)
        y = cen * lax.rsqrt(var + EPS) * lg_ref[...] + lb_ref[...]
        o_ref[c * CONV_ROWS:(c + 1) * CONV_ROWS, :] = (y * _sigmoid(y)).astype(o_ref.dtype)


def conv_module(u, conv_w, conv_b, ln_g, ln_b, *, tq=512):
    t = u.shape[0]
    tq = min(tq, t)
    c = CONV_CH
    hb = tq // CONV_HALO
    w = jnp.pad(conv_w.reshape(CONV_WIDTH, c), ((0, 1), (0, 0)))
    row = lambda v: v.reshape(1, c)
    const = lambda i: (0, 0)
    return pl.pallas_call(
        functools.partial(_conv_kernel, tq=tq),
        grid=(t // tq,),
        in_specs=[pl.BlockSpec((tq, c), lambda i: (i, 0)),
                  pl.BlockSpec((tq, c), lambda i: (i, 1)),
                  pl.BlockSpec((CONV_HALO, c), lambda i: (jnp.maximum(i * hb - 1, 0), 0)),
                  pl.BlockSpec((CONV_HALO, c), lambda i: (jnp.maximum(i * hb - 1, 0), 1)),
                  pl.BlockSpec((CONV_WIDTH + 1, c), const),
                  pl.BlockSpec((1, c), const), pl.BlockSpec((1, c), const), pl.BlockSpec((1, c), const)],
        out_specs=pl.BlockSpec((tq, c), lambda i: (i, 0)),
        out_shape=jax.ShapeDtypeStruct((t, c), MXU_DTYPE),
        scratch_shapes=[pltpu.VMEM((tq + CONV_HALO, c), F32)],
        compiler_params=_cparams(("arbitrary",)),
        name="conv_module",
    )(u, u, u, u, w, row(conv_b), row(ln_g), row(ln_b))


ONES_ROWS = 16
LOG2E = 1.4426950408889634
QK_SCALE_LOG2 = HEAD_DIM ** -0.5 * LOG2E


def _kv_prologue(k_ref, v_ref, kg_ref, kaug_s, vt_s, *, t, tk, blk):
    kn = _rms(k_ref[...], kg_ref[...])
    kaug_s[:, 0:HEAD_DIM] = kn.astype(kaug_s.dtype)
    if blk is not None:
        lane_blk = lax.broadcasted_iota(jnp.int32, (t, LANES), 1)
        key_blk = lax.broadcasted_iota(jnp.int32, (t, LANES), 0) // blk
        kaug_s[:, HEAD_DIM:] = jnp.where(lane_blk == key_blk, 1.0, 0.0).astype(kaug_s.dtype)
    for c in range(t // tk):
        vt_s[c, 0:HEAD_DIM, :] = v_ref[c * tk:(c + 1) * tk, :].T.astype(vt_s.dtype)
        vt_s[c, HEAD_DIM:, :] = jnp.ones((ONES_ROWS, tk), vt_s.dtype)
    return kn


PLAIN, CAUSAL, WINDOW_TAIL = 0, 1, 2


def _tile_pattern(kind, tk, tq):
    key_i = lax.broadcasted_iota(jnp.int32, (tk, tq), 0)
    qry_i = lax.broadcasted_iota(jnp.int32, (tk, tq), 1)
    if kind == CAUSAL:
        return key_i <= qry_i
    if kind == WINDOW_TAIL:
        return key_i > qry_i
    return None


def _fill_bias(sb_s, slopes, kinds, *, tk, tq):
    key_off = lax.broadcasted_iota(jnp.int32, (tk, tq), 0).astype(F32)
    for r, slope in enumerate(slopes):
        bias = (LOG2E * slope) * key_off
        for v, kind in enumerate(kinds):
            pattern = _tile_pattern(kind, tk, tq)
            sb_s[v, r] = bias if pattern is None else jnp.where(pattern, bias, NEG)


def _flash_init(m_s, acc_s):
    m_s[...] = jnp.full_like(m_s, NEG)
    acc_s[...] = jnp.zeros_like(acc_s)


def _qk_tile(j, slot, *, heads, qaug_s, kaug_s, s_s, tq, tk):
    kj = kaug_s[pl.ds(pl.multiple_of(j * tk, tk), tk), :]
    for r in range(heads):
        s_s[slot, r] = _dot_nt(kj, qaug_s[r * tq:(r + 1) * tq, :])


def _softmax_pv_tile(j, slot, variant, *, t0, slopes, vt_s, s_s, sb_s, m_s, acc_s, tq, tk):
    vtj = vt_s[j]
    off = (j * tk - t0).astype(F32)
    for r, slope in enumerate(slopes):
        cols = slice(r * tq, (r + 1) * tq)
        s_r = s_s[slot, r] + sb_s[variant, r]
        c = (LOG2E * slope) * off
        m_old = m_s[:, cols]
        m_new = jnp.maximum(m_old, jnp.max(s_r, axis=0, keepdims=True) + c)
        p_r = jnp.exp2(s_r - (m_new - c)).astype(MXU_DTYPE)
        acc_s[:, cols] = jnp.exp2(m_old - m_new) * acc_s[:, cols] + _dot(vtj, p_r)
        m_s[:, cols] = m_new


def _causal_sweep(qk, spv, tile_at, n_past):
    qk(tile_at(0), 0)

    def pair(i, carry):
        k = 2 * i
        qk(tile_at(k + 1), 1)
        spv(tile_at(k), 0, PLAIN)
        qk(tile_at(k + 2), 0)
        spv(tile_at(k + 1), 1, PLAIN)
        return carry

    lax.fori_loop(0, n_past // 2, pair, 0)
    k = (n_past // 2) * 2

    @pl.when(n_past % 2 == 1)
    def _():
        qk(tile_at(k + 1), 1)
        spv(tile_at(k), 0, PLAIN)
        spv(tile_at(k + 1), 1, CAUSAL)

    @pl.when(n_past % 2 == 0)
    def _():
        spv(tile_at(k), 0, CAUSAL)


def _flash_result(acc_s, r, tq):
    cols = slice(r * tq, (r + 1) * tq)
    o_t = acc_s[0:HEAD_DIM, cols] / acc_s[HEAD_DIM:HEAD_DIM + 1, cols]
    return o_t.T


def _flash_scratch(t, tq, tk, heads, kaug_cols, n_patterns, n_slots):
    nq = heads * tq
    return [pltpu.VMEM((t, kaug_cols), MXU_DTYPE),
            pltpu.VMEM((t // tk, HEAD_DIM + ONES_ROWS, tk), MXU_DTYPE),
            pltpu.VMEM((nq, kaug_cols), MXU_DTYPE),
            pltpu.VMEM((n_patterns, heads, tk, tq), F32),
            pltpu.VMEM((n_slots, heads, tk, tq), F32),
            pltpu.VMEM((1, nq), F32),
            pltpu.VMEM((HEAD_DIM + ONES_ROWS, nq), F32)]


def _flash_stages(*, t0, slopes, qaug_s, kaug_s, vt_s, sb_s, s_s, m_s, acc_s, tq, tk):
    qk = functools.partial(_qk_tile, heads=len(slopes), qaug_s=qaug_s, kaug_s=kaug_s, s_s=s_s, tq=tq, tk=tk)
    spv = functools.partial(_softmax_pv_tile, t0=t0, slopes=slopes, vt_s=vt_s, s_s=s_s, sb_s=sb_s, m_s=m_s,
                            acc_s=acc_s, tq=tq, tk=tk)
    return qk, spv


def _moba_kernel(slopes_ref, q_ref, k_ref, v_ref, qg_ref, kg_ref, o_ref,
                 kaug_s, vt_s, qaug_s, sb_s, s_s, m_s, acc_s, km_s, *, tq, t_total):
    h = pl.program_id(0)
    qi = pl.program_id(1)
    nb = t_total // MOBA_BLOCK
    tk = tq

    slopes = [slopes_ref[h]]

    @pl.when(qi == 0)
    def _():
        kn = _kv_prologue(k_ref, v_ref, kg_ref, kaug_s, vt_s, t=t_total, tk=tk, blk=MOBA_BLOCK)
        km_s[...] = jnp.zeros_like(km_s)
        km_s[0:nb, :] = jnp.mean(kn.reshape(nb, MOBA_BLOCK, HEAD_DIM), axis=1)
        _fill_bias(sb_s, slopes, (PLAIN, CAUSAL), tk=tk, tq=tq)

    qn = _rms(q_ref[...], qg_ref[...])
    nb8 = -(-nb // 8) * 8
    gate = _dot_nt(km_s[0:nb8, :], qn, precision=HIGHEST)
    blk = lax.broadcasted_iota(jnp.int32, (nb8, tq), 0)
    blk_f = blk.astype(F32)
    qb = (qi * tq + lax.broadcasted_iota(jnp.int32, (nb8, tq), 1)) // MOBA_BLOCK
    past = blk < qb
    gate = jnp.where(past, gate, -jnp.inf)
    sel = blk == qb
    for _ in range(MOBA_TOPK):
        top = jnp.max(gate, axis=0, keepdims=True)
        first = jnp.min(jnp.where(gate == top, blk_f, float(nb8)), axis=0, keepdims=True)
        pick = blk_f == first
        sel = sel | (pick & past)
        gate = jnp.where(pick, -jnp.inf, gate)
    sel_bias = jnp.where(sel, 0.0, NEG)
    if nb8 < LANES:
        sel_bias = jnp.concatenate([sel_bias, jnp.full((LANES - nb8, tq), NEG, F32)], axis=0)
    qaug_s[:, 0:HEAD_DIM] = (qn * QK_SCALE_LOG2).astype(qaug_s.dtype)
    qaug_s[:, HEAD_DIM:] = sel_bias.T.astype(qaug_s.dtype)

    _flash_init(m_s, acc_s)
    qk, spv = _flash_stages(t0=qi * tq, slopes=slopes, qaug_s=qaug_s, kaug_s=kaug_s, vt_s=vt_s, sb_s=sb_s,
                            s_s=s_s, m_s=m_s, acc_s=acc_s, tq=tq, tk=tk)
    _causal_sweep(qk, spv, lambda k: k, qi)
    o_ref[...] = _flash_result(acc_s, 0, tq).astype(o_ref.dtype)


def moba_attention(u, q_norm, k_norm, *, col0, tq=512):
    t = u.shape[0]
    tq = min(tq, t)
    assert t // MOBA_BLOCK <= LANES and tq % MOBA_BLOCK == 0
    nh = MOBA_HEADS
    slopes = jnp.asarray(2.0 ** (-8.0 * np.arange(1, nh + 1) / nh), dtype=F32)
    grid_spec = pltpu.PrefetchScalarGridSpec(
        num_scalar_prefetch=1,
        grid=(nh, t // tq),
        in_specs=[pl.BlockSpec((tq, HEAD_DIM), lambda h, i, s: (i, col0 + h)),
                  pl.BlockSpec((t, HEAD_DIM), lambda h, i, s: (0, col0 + nh + h)),
                  pl.BlockSpec((t, HEAD_DIM), lambda h, i, s: (0, col0 + 2 * nh + h)),
                  pl.BlockSpec((1, HEAD_DIM), lambda h, i, s: (0, 0)),
                  pl.BlockSpec((1, HEAD_DIM), lambda h, i, s: (0, 0))],
        out_specs=pl.BlockSpec((tq, HEAD_DIM), lambda h, i, s: (i, h)),
        scratch_shapes=_flash_scratch(t, tq, tq, 1, HEAD_DIM + LANES, 2, 2) + [pltpu.VMEM((LANES, HEAD_DIM), F32)])
    return pl.pallas_call(
        functools.partial(_moba_kernel, tq=tq, t_total=t),
        grid_spec=grid_spec,
        out_shape=jax.ShapeDtypeStruct((t, MOBA_DIM), MXU_DTYPE),
        compiler_params=_cparams(("arbitrary", "arbitrary")),
        name="moba_attention",
    )(slopes, u, u, u, q_norm.reshape(1, HEAD_DIM), k_norm.reshape(1, HEAD_DIM))


def _swiglu_step(h, wg_ref, wu_ref, wd_ref):
    a = _dot(h, wg_ref[...])
    b = _dot(h, wu_ref[...])
    mid = (a * _sigmoid(a) * b).astype(MXU_DTYPE)
    return _dot(mid, wd_ref[...])


def _dense_ffn_kernel(x_ref, g_ref, wg_ref, wu_ref, wd_ref, o_ref, h_ref):
    @pl.when(pl.program_id(1) == 0)
    def _():
        x = x_ref[...]
        h_ref[...] = _rms(x, g_ref[...]).astype(h_ref.dtype)
        o_ref[...] = x

    o_ref[...] += _swiglu_step(h_ref[...], wg_ref, wu_ref, wd_ref)


def dense_ffn(x, g, wg, wu, wd, *, tm=512, tf=512):
    t, d = x.shape
    f = wg.shape[1]
    tm = min(tm, t)
    return pl.pallas_call(
        _dense_ffn_kernel,
        grid=(t // tm, f // tf),
        in_specs=[pl.BlockSpec((tm, d), lambda i, j: (i, 0)),
                  pl.BlockSpec((1, d), lambda i, j: (0, 0)),
                  pl.BlockSpec((d, tf), lambda i, j: (0, j)),
                  pl.BlockSpec((d, tf), lambda i, j: (0, j)),
                  pl.BlockSpec((tf, d), lambda i, j: (j, 0))],
        out_specs=pl.BlockSpec((tm, d), lambda i, j: (i, 0)),
        out_shape=jax.ShapeDtypeStruct((t, d), F32),
        scratch_shapes=[pltpu.VMEM((tm, d), MXU_DTYPE)],
        compiler_params=_cparams(("arbitrary", "arbitrary")),
        name="dense_ffn",
    )(x, g.reshape(1, d), wg, wu, wd)


def _grouped_ffn_kernel(te_ref, rows_ref, x_ref, wg_ref, wu_ref, wd_ref, o_ref, h_ref, *, tm):
    rows = rows_ref[pl.program_id(0)]
    half = tm // 2

    @pl.when(pl.program_id(1) == 0)
    def _():
        h_ref[...] = x_ref[...].astype(h_ref.dtype)
        o_ref[...] = jnp.zeros_like(o_ref)

    @pl.when(rows > half)
    def _():
        o_ref[...] += _swiglu_step(h_ref[...], wg_ref, wu_ref, wd_ref)

    @pl.when((rows > 0) & (rows <= half))
    def _():
        o_ref[0:half, :] += _swiglu_step(h_ref[0:half, :], wg_ref, wu_ref, wd_ref)


def grouped_ffn(xs, tile_expert, tile_rows, wg, wu, wd, *, tm, tf=512):
    m, d = xs.shape
    f = wg.shape[2]
    nf = f // tf

    def fcol(i, j, rows):
        return jnp.where(rows[i] > 0, j, nf - 1)

    grid_spec = pltpu.PrefetchScalarGridSpec(
        num_scalar_prefetch=2,
        grid=(m // tm, nf),
        in_specs=[pl.BlockSpec((tm, d), lambda i, j, te, rows: (i, 0)),
                  pl.BlockSpec((None, d, tf), lambda i, j, te, rows: (te[i], 0, fcol(i, j, rows))),
                  pl.BlockSpec((None, d, tf), lambda i, j, te, rows: (te[i], 0, fcol(i, j, rows))),
                  pl.BlockSpec((None, tf, d), lambda i, j, te, rows: (te[i], fcol(i, j, rows), 0))],
        out_specs=pl.BlockSpec((tm, d), lambda i, j, te, rows: (i, 0)),
        scratch_shapes=[pltpu.VMEM((tm, d), MXU_DTYPE)])
    return pl.pallas_call(
        functools.partial(_grouped_ffn_kernel, tm=tm),
        grid_spec=grid_spec,
        out_shape=jax.ShapeDtypeStruct((m, d), F32),
        compiler_params=_cparams(("arbitrary", "arbitrary")),
        name="grouped_ffn",
    )(tile_expert, tile_rows, xs, wg, wu, wd)


META_E, META_W, META_RANK = 0, 2, 4


def _router_kernel(x_ref, g_ref, wr_ref, h_ref, meta_ref, cnt_ref, carry_s, *, tm):
    @pl.when(pl.program_id(0) == 0)
    def _():
        carry_s[...] = jnp.zeros_like(carry_s)

    h = _rms(x_ref[...], g_ref[...])
    h_ref[...] = h
    logits = jnp.dot(h, wr_ref[...], precision=HIGHEST, preferred_element_type=F32)
    lane = lax.broadcasted_iota(jnp.int32, (tm, LANES), 1).astype(F32)
    lg = jnp.where(lane < N_EXPERTS, logits, -jnp.inf)

    def take_top(v):
        top = jnp.max(v, axis=-1, keepdims=True)
        idx = jnp.min(jnp.where(v == top, lane, float(LANES)), axis=-1, keepdims=True)
        return top, idx

    m1, i1 = take_top(lg)
    m2, i2 = take_top(jnp.where(lane == i1, -jnp.inf, lg))
    e = jnp.exp(m2 - m1)
    w1 = 1.0 / (1.0 + e)
    w2 = e / (1.0 + e)
    oh1 = lane == i1
    oh2 = lane == i2
    oh = jnp.where(oh1 | oh2, 1.0, 0.0)
    r = lax.broadcasted_iota(jnp.int32, (tm, tm), 0)
    c = lax.broadcasted_iota(jnp.int32, (tm, tm), 1)
    lower = jnp.where(c < r, 1.0, 0.0).astype(BF16)
    before = _dot(lower, oh.astype(BF16)) + carry_s[...]
    rank1 = jnp.sum(jnp.where(oh1, before, 0.0), axis=-1, keepdims=True)
    rank2 = jnp.sum(jnp.where(oh2, before, 0.0), axis=-1, keepdims=True)
    carry_s[...] += jnp.sum(oh, axis=0, keepdims=True)
    meta = jnp.zeros((tm, LANES), F32)
    for k, v in ((META_E, i1), (META_E + 1, i2), (META_W, w1), (META_W + 1, w2),
                 (META_RANK, rank1), (META_RANK + 1, rank2)):
        meta = jnp.where(lane == k, v, meta)
    meta_ref[...] = meta
    cnt_ref[...] = jnp.broadcast_to(carry_s[...], cnt_ref.shape)


def moe_router(x, g, w_router, *, tm=512):
    t, d = x.shape
    tm = min(tm, t)
    wr = jnp.pad(w_router, ((0, 0), (0, LANES - N_EXPERTS)))
    return pl.pallas_call(
        functools.partial(_router_kernel, tm=tm),
        grid=(t // tm,),
        in_specs=[pl.BlockSpec((tm, d), lambda i: (i, 0)),
                  pl.BlockSpec((1, d), lambda i: (0, 0)),
                  pl.BlockSpec((d, LANES), lambda i: (0, 0))],
        out_specs=[pl.BlockSpec((tm, d), lambda i: (i, 0)),
                   pl.BlockSpec((tm, LANES), lambda i: (i, 0)),
                   pl.BlockSpec((8, LANES), lambda i: (0, 0))],
        out_shape=[jax.ShapeDtypeStruct((t, d), F32),
                   jax.ShapeDtypeStruct((t, LANES), F32),
                   jax.ShapeDtypeStruct((8, LANES), F32)],
        scratch_shapes=[pltpu.VMEM((1, LANES), F32)],
        compiler_params=_cparams(("arbitrary",)),
        name="moe_router",
    )(x, g.reshape(1, d), wr)


def _dispatch_kernel(p1_ref, p2_ref, h_ref, init_ref, xs_ref, sem, *, tr):
    del init_ref
    base = pl.program_id(0) * tr

    def copies(r, tok):
        src = h_ref.at[pl.ds(r, 1)]
        return (pltpu.make_async_copy(src, xs_ref.at[pl.ds(p1_ref[tok], 1)], sem.at[0]),
                pltpu.make_async_copy(src, xs_ref.at[pl.ds(p2_ref[tok], 1)], sem.at[1]))

    def issue(r, carry):
        for c in copies(r, base + r):
            c.start()
        return carry

    lax.fori_loop(0, tr, issue, 0)

    def drain(r, carry):
        for c in copies(0, 0):
            c.wait()
        return carry

    lax.fori_loop(0, tr, drain, 0)


def moe_dispatch(h, pos1, pos2, m_pad, *, tr=256):
    t, d = h.shape
    tr = min(tr, t)
    grid_spec = pltpu.PrefetchScalarGridSpec(
        num_scalar_prefetch=2,
        grid=(t // tr,),
        in_specs=[pl.BlockSpec((tr, d), lambda i, p1, p2: (i, 0)),
                  pl.BlockSpec(memory_space=pl.ANY)],
        out_specs=pl.BlockSpec(memory_space=pl.ANY),
        scratch_shapes=[pltpu.SemaphoreType.DMA((2,))])
    return pl.pallas_call(
        functools.partial(_dispatch_kernel, tr=tr),
        grid_spec=grid_spec,
        out_shape=jax.ShapeDtypeStruct((m_pad, d), h.dtype),
        input_output_aliases={3: 0},
        compiler_params=_cparams(("arbitrary",)),
        name="moe_dispatch",
    )(pos1, pos2, h, jnp.zeros((m_pad, d), h.dtype))


def _combine_kernel(p1_ref, p2_ref, x_ref, meta_ref, y_ref, o_ref, y1_s, y2_s, sem, *, tr):
    base = pl.program_id(0) * tr

    def copies(r, tok):
        return (pltpu.make_async_copy(y_ref.at[pl.ds(p1_ref[tok], 1)], y1_s.at[pl.ds(r, 1)], sem.at[0]),
                pltpu.make_async_copy(y_ref.at[pl.ds(p2_ref[tok], 1)], y2_s.at[pl.ds(r, 1)], sem.at[1]))

    def issue(r, carry):
        for c in copies(r, base + r):
            c.start()
        return carry

    lax.fori_loop(0, tr, issue, 0)

    def drain(r, carry):
        for c in copies(0, 0):
            c.wait()
        return carry

    lax.fori_loop(0, tr, drain, 0)
    meta = meta_ref[...]
    w1 = meta[:, META_W:META_W + 1]
    w2 = meta[:, META_W + 1:META_W + 2]
    o_ref[...] = x_ref[...] + w1 * y1_s[...] + w2 * y2_s[...]


def moe_combine(x, meta, y, pos1, pos2, *, tr=256):
    t, d = x.shape
    tr = min(tr, t)
    grid_spec = pltpu.PrefetchScalarGridSpec(
        num_scalar_prefetch=2,
        grid=(t // tr,),
        in_specs=[pl.BlockSpec((tr, d), lambda i, p1, p2: (i, 0)),
                  pl.BlockSpec((tr, LANES), lambda i, p1, p2: (i, 0)),
                  pl.BlockSpec(memory_space=pl.ANY)],
        out_specs=pl.BlockSpec((tr, d), lambda i, p1, p2: (i, 0)),
        scratch_shapes=[pltpu.VMEM((tr, d), F32), pltpu.VMEM((tr, d), F32),
                        pltpu.SemaphoreType.DMA((2,))])
    return pl.pallas_call(
        functools.partial(_combine_kernel, tr=tr),
        grid_spec=grid_spec,
        out_shape=jax.ShapeDtypeStruct((t, d), F32),
        compiler_params=_cparams(("arbitrary",)),
        name="moe_combine",
    )(pos1, pos2, x, meta, y)


def moe_ffn(x, ffn_norm, w_router, wg, wu, wd, *, tm=512):
    t, d = x.shape
    tm = min(tm, t)
    h, meta, cnt = moe_router(x, ffn_norm, w_router)
    counts = cnt[0, :N_EXPERTS].astype(jnp.int32)
    padded = (counts + tm - 1) // tm * tm
    ends = jnp.cumsum(padded)
    starts = ends - padded
    ids = meta[:, META_E:META_E + 2].astype(jnp.int32)
    ranks = meta[:, META_RANK:META_RANK + 2].astype(jnp.int32)
    pos = starts[ids] + ranks
    pos1, pos2 = pos[:, 0], pos[:, 1]
    m_pad = TOP_K * t + N_EXPERTS * tm
    n_tiles = m_pad // tm
    tile_start = jnp.arange(n_tiles, dtype=jnp.int32) * tm
    tile_expert = jnp.sum((tile_start[:, None] >= ends[None, :]).astype(jnp.int32), axis=1)
    last_expert = jnp.sum((ends[-1] - 1 >= ends).astype(jnp.int32))
    used = tile_start < ends[-1]
    tile_expert = jnp.where(used, tile_expert, last_expert).astype(jnp.int32)
    tile_rows = jnp.clip((starts + counts)[tile_expert] - tile_start, 0, tm)
    tile_rows = jnp.where(used, tile_rows, 0).astype(jnp.int32)
    xs = moe_dispatch(h, pos1, pos2, m_pad)
    y = grouped_ffn(xs, tile_expert, tile_rows, wg, wu, wd, tm=tm)
    return moe_combine(x, meta, y, pos1, pos2)


def _compress_kernel(c_ref, pe_ref, w1_ref, w2_ref, ng_ref, o_ref, *, nrows, half, normalize):
    c = c_ref[...].astype(MXU_DTYPE)
    top = _dot(c, w1_ref[0:half, :])
    bot = _dot(c, w1_ref[half:, :])
    bot_next = pltpu.roll(bot, nrows - 1, 0)
    pe_term = _dot(pe_ref[...].astype(MXU_DTYPE), w1_ref[...])
    hid = top + bot_next + pe_term
    out = _dot((hid * _sigmoid(hid)).astype(MXU_DTYPE), w2_ref[...])
    if normalize:
        out = _rms(out, ng_ref[...])
    valid = lax.broadcasted_iota(jnp.int32, (nrows, 1), 0) < nrows - 1
    o_ref[...] = jnp.where(valid, out, 0.0)


def compress(c, pe, w1, w2, norm_g, *, normalize):
    g, nrows, half = c.shape
    return pl.pallas_call(
        functools.partial(_compress_kernel, nrows=nrows, half=half, normalize=normalize),
        grid=(g,),
        in_specs=[pl.BlockSpec((None, nrows, half), lambda i: (i, 0, 0)),
                  pl.BlockSpec((1, 2 * half), lambda i: (0, 0)),
                  pl.BlockSpec((2 * half, CMP_HIDDEN), lambda i: (0, 0)),
                  pl.BlockSpec((CMP_HIDDEN, HEAD_DIM), lambda i: (0, 0)),
                  pl.BlockSpec((1, HEAD_DIM), lambda i: (0, 0))],
        out_specs=pl.BlockSpec((None, nrows, HEAD_DIM), lambda i: (i, 0, 0)),
        out_shape=jax.ShapeDtypeStruct((g, nrows, HEAD_DIM), F32),
        compiler_params=_cparams(("arbitrary",)),
        name="nsa_compress",
    )(c, pe.reshape(1, 2 * half), w1, w2, norm_g.reshape(1, HEAD_DIM))


def _nsa_cmp_kernel(slopes_ref, q_ref, kc_ref, vc_ref, qg_ref, ov_ref, oc_ref, sel_ref, used_ref, kc_s, vct_s,
                    psum_s, *, tq, ncmp, nsel):
    g = pl.program_id(0)
    qi = pl.program_id(1)

    @pl.when(qi == 0)
    def _():
        kc_s[...] = kc_ref[...].astype(kc_s.dtype)
        vct_s[...] = vc_ref[...].T.astype(vct_s.dtype)

    def heads(nk):
        tpos = qi * tq + lax.broadcasted_iota(jnp.int32, (nk, tq), 1)
        cend = lax.broadcasted_iota(jnp.int32, (nk, tq), 0) * CMP_STRIDE + (CMP_LEN - 1)
        dist = tpos - cend
        ok = dist >= 0
        dist_f = dist.astype(F32)
        any_ok = jnp.where(tpos[0:1, :] >= CMP_LEN - 1, 1.0, 0.0)
        psum = jnp.zeros((nk, tq), F32)
        for r in range(NSA_GROUP):
            cols = slice(r * HEAD_DIM, (r + 1) * HEAD_DIM)
            qn = _rms(q_ref[:, cols], qg_ref[...])
            s = _dot_nt(kc_s[0:nk, :], (qn * HEAD_DIM ** -0.5).astype(MXU_DTYPE))
            s = jnp.where(ok, s - slopes_ref[g * NSA_GROUP + r] * dist_f, NEG)
            p = jnp.exp(s - jnp.max(s, axis=0, keepdims=True))
            p = p * (any_ok / jnp.sum(p, axis=0, keepdims=True))
            oc_ref[:, cols] = _dot(vct_s[:, 0:nk], p.astype(MXU_DTYPE)).T.astype(oc_ref.dtype)
            psum = psum + p
        psum_s[0:nk, :] = psum
        if nk < ncmp:
            psum_s[nk:, :] = jnp.zeros((ncmp - nk, tq), F32)

    n_prefix = ncmp // LANES if ncmp % LANES == 0 else 1
    step_keys = ncmp // n_prefix
    n_visible = (qi * tq + tq - CMP_LEN) // CMP_STRIDE + 1
    need = jnp.minimum((n_visible + step_keys - 1) // step_keys, n_prefix)
    for v in range(1, n_prefix + 1):
        @pl.when(need == v)
        def _():
            heads(v * step_keys)

    psum = psum_s[...]
    p_hi = psum.astype(MXU_DTYPE)
    p_lo = (psum - p_hi.astype(F32)).astype(MXU_DTYPE)
    imp = _dot(ov_ref[...], p_hi) + _dot(ov_ref[...], p_lo)
    blk = lax.broadcasted_iota(jnp.int32, (LANES, tq), 0)
    qb = (qi * tq + lax.broadcasted_iota(jnp.int32, (LANES, tq), 1)) // SLC_BLOCK
    forced = (blk == 0) | (blk == qb) | (blk == qb - 1)
    visible = blk <= qb
    imp = jnp.where(visible & jnp.logical_not(forced), imp, -jnp.inf)
    blk_f = blk.astype(F32)
    sel = forced
    for _ in range(min(SLC_TOPN, nsel) - 3):
        top = jnp.max(imp, axis=0, keepdims=True)
        first = jnp.min(jnp.where(imp == top, blk_f, float(LANES)), axis=0, keepdims=True)
        pick = blk_f == first
        sel = sel | pick
        imp = jnp.where(pick, -jnp.inf, imp)
    sel_q = jnp.where(sel & visible, 1.0, 0.0).T
    sel_ref[...] = sel_q.astype(sel_ref.dtype)
    used_ref[...] = jnp.broadcast_to(jnp.max(sel_q, axis=0, keepdims=True), used_ref.shape)


def nsa_compressed(u, k_cmp, v_cmp, q_norm, *, tq=256):
    t = u.shape[0]
    tq = min(tq, t)
    ncmp = k_cmp.shape[1]
    nsel = t // SLC_BLOCK
    assert nsel <= LANES
    nstart = np.arange(ncmp)[:, None] * CMP_STRIDE
    lo = np.arange(LANES)[None, :] * SLC_BLOCK
    overlap = ((nstart < lo + SLC_BLOCK) & (nstart + CMP_LEN > lo) & (np.arange(ncmp)[:, None] < ncmp - 1))
    ov_t = jnp.asarray(overlap.T.astype(np.float32)).astype(MXU_DTYPE)
    gw = NSA_GROUP * HEAD_DIM
    grid_spec = pltpu.PrefetchScalarGridSpec(
        num_scalar_prefetch=1,
        grid=(NSA_KV_HEADS, t // tq),
        in_specs=[pl.BlockSpec((tq, gw), lambda g, i, s: (i, g)),
                  pl.BlockSpec((None, ncmp, HEAD_DIM), lambda g, i, s: (g, 0, 0)),
                  pl.BlockSpec((None, ncmp, HEAD_DIM), lambda g, i, s: (g, 0, 0)),
                  pl.BlockSpec((1, HEAD_DIM), lambda g, i, s: (0, 0)),
                  pl.BlockSpec((LANES, ncmp), lambda g, i, s: (0, 0))],
        out_specs=[pl.BlockSpec((tq, gw), lambda g, i, s: (i, g)),
                   pl.BlockSpec((None, tq, LANES), lambda g, i, s: (g, i, 0)),
                   pl.BlockSpec((None, None, 8, LANES), lambda g, i, s: (g, i, 0, 0))],
        scratch_shapes=[pltpu.VMEM((ncmp, HEAD_DIM), MXU_DTYPE), pltpu.VMEM((HEAD_DIM, ncmp), MXU_DTYPE),
                        pltpu.VMEM((ncmp, tq), F32)])
    return pl.pallas_call(
        functools.partial(_nsa_cmp_kernel, tq=tq, ncmp=ncmp, nsel=nsel),
        grid_spec=grid_spec,
        out_shape=[jax.ShapeDtypeStruct((t, NSA_DIM), F32),
                   jax.ShapeDtypeStruct((NSA_KV_HEADS, t, LANES), MXU_DTYPE),
                   jax.ShapeDtypeStruct((NSA_KV_HEADS, t // tq, 8, LANES), F32)],
        compiler_params=_cparams(("arbitrary", "arbitrary")),
        name="nsa_compressed",
    )(_nsa_slopes(), u, k_cmp, v_cmp, q_norm.reshape(1, HEAD_DIM), ov_t)


def _nsa_slopes():
    return jnp.asarray(2.0 ** (-8.0 * np.arange(1, NSA_HEADS + 1) / NSA_HEADS), dtype=F32)


def _group_queries(q_ref, qg_ref, qaug_s, tq):
    for r in range(NSA_GROUP):
        qn = _rms(q_ref[:, r * HEAD_DIM:(r + 1) * HEAD_DIM], qg_ref[...])
        qaug_s[r * tq:(r + 1) * tq, 0:HEAD_DIM] = (qn * QK_SCALE_LOG2).astype(qaug_s.dtype)


def _nsa_slc_kernel(slopes_ref, tiles_ref, npast_ref, q_ref, k_ref, v_ref, sel_ref, qg_ref, kg_ref, o_ref,
                    kaug_s, vt_s, qaug_s, sb_s, s_s, m_s, acc_s, *, tq, t_total):
    g = pl.program_id(0)
    qi = pl.program_id(1)
    tk = tq
    row_len = t_total // tk + 1
    slopes = [slopes_ref[g * NSA_GROUP + r] for r in range(NSA_GROUP)]

    @pl.when(qi == 0)
    def _():
        _kv_prologue(k_ref, v_ref, kg_ref, kaug_s, vt_s, t=t_total, tk=tk, blk=SLC_BLOCK)
        _fill_bias(sb_s, slopes, (PLAIN, CAUSAL), tk=tk, tq=tq)

    _group_queries(q_ref, qg_ref, qaug_s, tq)
    sel_bias = jnp.where(sel_ref[...].astype(F32) > 0.5, 0.0, NEG).astype(qaug_s.dtype)
    for r in range(NSA_GROUP):
        qaug_s[r * tq:(r + 1) * tq, HEAD_DIM:] = sel_bias
    _flash_init(m_s, acc_s)
    qk, spv = _flash_stages(t0=qi * tq, slopes=slopes, qaug_s=qaug_s, kaug_s=kaug_s, vt_s=vt_s, sb_s=sb_s,
                            s_s=s_s, m_s=m_s, acc_s=acc_s, tq=tq, tk=tk)
    row = g * pl.num_programs(1) + qi
    _causal_sweep(qk, spv, lambda k: tiles_ref[row * row_len + k], npast_ref[row])
    for r in range(NSA_GROUP):
        o_ref[:, r * HEAD_DIM:(r + 1) * HEAD_DIM] = _flash_result(acc_s, r, tq).astype(o_ref.dtype)


def _nsa_win_kernel(slopes_ref, q_ref, k_ref, v_ref, oc_ref, os_ref, gl_ref, qg_ref, kg_ref, o_ref,
                    kaug_s, vt_s, qaug_s, sb_s, s_s, m_s, acc_s, *, tq, t_total):
    g = pl.program_id(0)
    qi = pl.program_id(1)
    tk = tq
    nw = WINDOW // tk
    slopes = [slopes_ref[g * NSA_GROUP + r] for r in range(NSA_GROUP)]

    @pl.when(qi == 0)
    def _():
        _kv_prologue(k_ref, v_ref, kg_ref, kaug_s, vt_s, t=t_total, tk=tk, blk=None)
        _fill_bias(sb_s, slopes, (PLAIN, CAUSAL, WINDOW_TAIL), tk=tk, tq=tq)

    _group_queries(q_ref, qg_ref, qaug_s, tq)
    _flash_init(m_s, acc_s)
    qk, spv = _flash_stages(t0=qi * tq, slopes=slopes, qaug_s=qaug_s, kaug_s=kaug_s, vt_s=vt_s, sb_s=sb_s,
                            s_s=s_s, m_s=m_s, acc_s=acc_s, tq=tq, tk=tk)
    kinds = [WINDOW_TAIL] + [PLAIN] * (nw - 1) + [CAUSAL]

    @pl.when(qi >= nw)
    def _():
        for i in range(nw + 1):
            qk(qi - nw + i, i)
        for i, kind in enumerate(kinds):
            spv(qi - nw + i, i, kind)

    @pl.when(qi < nw)
    def _():
        for back in range(nw - 1, -1, -1):
            @pl.when(qi >= back)
            def _():
                qk(qi - back, 0)
                spv(qi - back, 0, kinds[nw - back])
    gates = _sigmoid(gl_ref[...])
    for r in range(NSA_GROUP):
        cols = slice(r * HEAD_DIM, (r + 1) * HEAD_DIM)
        lane0 = 3 * r
        o = (gates[:, lane0:lane0 + 1] * oc_ref[:, cols] + gates[:, lane0 + 1:lane0 + 2] * os_ref[:, cols]
             + gates[:, lane0 + 2:lane0 + 3] * _flash_result(acc_s, r, tq))
        o_ref[:, cols] = o.astype(o_ref.dtype)


def nsa_selected(u, sel, blocks_used, q_norm, k_norm, *, kcol, vcol, tq=256):
    t = u.shape[0]
    tq = min(tq, t)
    gw = NSA_GROUP * HEAD_DIM
    per_tile = tq // SLC_BLOCK
    nt = t // tq
    used = blocks_used[:, :, 0, :t // SLC_BLOCK].reshape(NSA_KV_HEADS, nt, nt, per_tile)
    tile_ids = jnp.arange(nt, dtype=jnp.int32)
    past_used = (jnp.max(used, axis=-1) > 0) & (tile_ids[None, None, :] < tile_ids[None, :, None])
    n_past = jnp.sum(past_used, axis=-1).astype(jnp.int32)
    order = jnp.argsort(jnp.logical_not(past_used), axis=-1, stable=True).astype(jnp.int32)
    slots = jnp.arange(nt + 1, dtype=jnp.int32)
    order = jnp.concatenate([order, order[..., :1]], axis=-1)
    tiles = jnp.where(slots < n_past[..., None], order, tile_ids[None, :, None])
    grid_spec = pltpu.PrefetchScalarGridSpec(
        num_scalar_prefetch=3,
        grid=(NSA_KV_HEADS, nt),
        in_specs=[pl.BlockSpec((tq, gw), lambda g, i, s, tl, n: (i, g)),
                  pl.BlockSpec((t, HEAD_DIM), lambda g, i, s, tl, n: (0, kcol + g)),
                  pl.BlockSpec((t, HEAD_DIM), lambda g, i, s, tl, n: (0, vcol + g)),
                  pl.BlockSpec((None, tq, LANES), lambda g, i, s, tl, n: (g, i, 0)),
                  pl.BlockSpec((1, HEAD_DIM), lambda g, i, s, tl, n: (0, 0)),
                  pl.BlockSpec((1, HEAD_DIM), lambda g, i, s, tl, n: (0, 0))],
        out_specs=pl.BlockSpec((tq, gw), lambda g, i, s, tl, n: (i, g)),
        scratch_shapes=_flash_scratch(t, tq, tq, NSA_GROUP, HEAD_DIM + LANES, 2, 2))
    return pl.pallas_call(
        functools.partial(_nsa_slc_kernel, tq=tq, t_total=t),
        grid_spec=grid_spec,
        out_shape=jax.ShapeDtypeStruct((t, NSA_DIM), F32),
        compiler_params=_cparams(("arbitrary", "arbitrary")),
        name="nsa_selected",
    )(_nsa_slopes(), tiles.reshape(-1), n_past.reshape(-1), u, u, u, sel,
      q_norm.reshape(1, HEAD_DIM), k_norm.reshape(1, HEAD_DIM))


def nsa_window_merge(u, o_cmp, o_slc, gate_logits, q_norm, k_norm, *, kcol, vcol, tq=256):
    t = u.shape[0]
    tq = min(tq, t)
    assert WINDOW % tq == 0
    gw = NSA_GROUP * HEAD_DIM
    gl = gate_logits.reshape(t, NSA_KV_HEADS, NSA_GROUP * 3)
    gl = jnp.pad(gl, ((0, 0), (0, 0), (0, LANES - NSA_GROUP * 3))).reshape(t, NSA_KV_HEADS * LANES)
    grid_spec = pltpu.PrefetchScalarGridSpec(
        num_scalar_prefetch=1,
        grid=(NSA_KV_HEADS, t // tq),
        in_specs=[pl.BlockSpec((tq, gw), lambda g, i, s: (i, g)),
                  pl.BlockSpec((t, HEAD_DIM), lambda g, i, s: (0, kcol + g)),
                  pl.BlockSpec((t, HEAD_DIM), lambda g, i, s: (0, vcol + g)),
                  pl.BlockSpec((tq, gw), lambda g, i, s: (i, g)),
                  pl.BlockSpec((tq, gw), lambda g, i, s: (i, g)),
                  pl.BlockSpec((tq, LANES), lambda g, i, s: (i, g)),
                  pl.BlockSpec((1, HEAD_DIM), lambda g, i, s: (0, 0)),
                  pl.BlockSpec((1, HEAD_DIM), lambda g, i, s: (0, 0))],
        out_specs=pl.BlockSpec((tq, gw), lambda g, i, s: (i, g)),
        scratch_shapes=_flash_scratch(t, tq, tq, NSA_GROUP, HEAD_DIM, 3, WINDOW // tq + 1))
    return pl.pallas_call(
        functools.partial(_nsa_win_kernel, tq=tq, t_total=t),
        grid_spec=grid_spec,
        out_shape=jax.ShapeDtypeStruct((t, NSA_DIM), MXU_DTYPE),
        compiler_params=_cparams(("arbitrary", "arbitrary")),
        name="nsa_window_merge",
    )(_nsa_slopes(), u, u, u, o_cmp, o_slc, gl, q_norm.reshape(1, HEAD_DIM), k_norm.reshape(1, HEAD_DIM))


def _mx(w):
    return w.astype(MXU_DTYPE)


def conv_moba_layer(x, attn_norm, w_in, conv_w, conv_b, conv_norm_g, conv_norm_b, q_norm, k_norm,
                    w_out, ffn_norm, w_gate, w_up, w_down):
    u = norm_matmul(x, attn_norm, _mx(w_in))
    a = conv_module(u, conv_w, conv_b, conv_norm_g, conv_norm_b)
    o = moba_attention(u, q_norm, k_norm, col0=2 * CONV_CH // HEAD_DIM)
    w_out = _mx(w_out)
    x = proj_residual(x, [(a, w_out[:CONV_CH]), (o, w_out[CONV_CH:])])
    return dense_ffn(x, ffn_norm, _mx(w_gate), _mx(w_up), _mx(w_down))


def nsa_moe_layer(x, attn_norm, w_in, q_norm, kc_norm, ks_norm, kw_norm, cmp_pe_k, cmp_w1_k, cmp_w2_k,
                  cmp_pe_v, cmp_w1_v, cmp_w2_v, w_out, ffn_norm, w_router, w_gate, w_up, w_down):
    t = x.shape[0]
    main = NSA_DIM + 6 * NSA_KV_DIM
    n_gate = 3 * NSA_HEADS
    gate_tile = 512
    w_in_p = jnp.pad(_mx(w_in), ((0, 0), (0, gate_tile - n_gate)))
    u = norm_matmul(x, attn_norm, w_in_p, tn=gate_tile)
    gate_logits = u[:, main:main + n_gate]
    nrows = t // CMP_STRIDE

    def blocks(col):
        v = u[:, col:col + NSA_KV_DIM].reshape(nrows, CMP_STRIDE, NSA_KV_HEADS, HEAD_DIM)
        return v.transpose(2, 0, 1, 3).reshape(NSA_KV_HEADS, nrows, CMP_STRIDE * HEAD_DIM)

    k_cmp = compress(blocks(NSA_DIM), cmp_pe_k, _mx(cmp_w1_k), _mx(cmp_w2_k), kc_norm, normalize=True)
    v_cmp = compress(blocks(NSA_DIM + NSA_KV_DIM), cmp_pe_v, _mx(cmp_w1_v), _mx(cmp_w2_v), kc_norm,
                     normalize=False)
    o_cmp, sel, blocks_used = nsa_compressed(u, k_cmp, v_cmp, q_norm)
    cb = lambda col: col // HEAD_DIM
    o_slc = nsa_selected(u, sel, blocks_used, q_norm, ks_norm, kcol=cb(NSA_DIM + 2 * NSA_KV_DIM),
                         vcol=cb(NSA_DIM + 3 * NSA_KV_DIM))
    o = nsa_window_merge(u, o_cmp, o_slc, gate_logits, q_norm, kw_norm,
                         kcol=cb(NSA_DIM + 4 * NSA_KV_DIM), vcol=cb(NSA_DIM + 5 * NSA_KV_DIM))
    x = proj_residual(x, [(o, _mx(w_out))])
    return moe_ffn(x, ffn_norm, w_router, _mx(w_gate), _mx(w_up), _mx(w_down))


def kernel(x, l0_attn_norm, l0_w_in, l0_conv_w, l0_conv_b, l0_conv_norm_g, l0_conv_norm_b, l0_q_norm, l0_k_norm, l0_w_out, l0_ffn_norm, l0_w_gate, l0_w_up, l0_w_down, l1_attn_norm, l1_w_in, l1_q_norm, l1_kc_norm, l1_ks_norm, l1_kw_norm, l1_cmp_pe_k, l1_cmp_w1_k, l1_cmp_w2_k, l1_cmp_pe_v, l1_cmp_w1_v, l1_cmp_w2_v, l1_w_out, l1_ffn_norm, l1_w_router, l1_w_gate, l1_w_up, l1_w_down, l2_attn_norm, l2_w_in, l2_conv_w, l2_conv_b, l2_conv_norm_g, l2_conv_norm_b, l2_q_norm, l2_k_norm, l2_w_out, l2_ffn_norm, l2_w_gate, l2_w_up, l2_w_down, l3_attn_norm, l3_w_in, l3_q_norm, l3_kc_norm, l3_ks_norm, l3_kw_norm, l3_cmp_pe_k, l3_cmp_w1_k, l3_cmp_w2_k, l3_cmp_pe_v, l3_cmp_w1_v, l3_cmp_w2_v, l3_w_out, l3_ffn_norm, l3_w_router, l3_w_gate, l3_w_up, l3_w_down):
    b, t, d = x.shape
    assert b == 1 and d == D_MODEL
    h = x.reshape(t, d)
    h = conv_moba_layer(h, l0_attn_norm, l0_w_in, l0_conv_w, l0_conv_b, l0_conv_norm_g, l0_conv_norm_b, l0_q_norm, l0_k_norm, l0_w_out, l0_ffn_norm, l0_w_gate, l0_w_up, l0_w_down)
    h = nsa_moe_layer(h, l1_attn_norm, l1_w_in, l1_q_norm, l1_kc_norm, l1_ks_norm, l1_kw_norm, l1_cmp_pe_k, l1_cmp_w1_k, l1_cmp_w2_k, l1_cmp_pe_v, l1_cmp_w1_v, l1_cmp_w2_v, l1_w_out, l1_ffn_norm, l1_w_router, l1_w_gate, l1_w_up, l1_w_down)
    h = conv_moba_layer(h, l2_attn_norm, l2_w_in, l2_conv_w, l2_conv_b, l2_conv_norm_g, l2_conv_norm_b, l2_q_norm, l2_k_norm, l2_w_out, l2_ffn_norm, l2_w_gate, l2_w_up, l2_w_down)
    h = nsa_moe_layer(h, l3_attn_norm, l3_w_in, l3_q_norm, l3_kc_norm, l3_ks_norm, l3_kw_norm, l3_cmp_pe_k, l3_cmp_w1_k, l3_cmp_w2_k, l3_cmp_pe_v, l3_cmp_w1_v, l3_cmp_w2_v, l3_w_out, l3_ffn_norm, l3_w_router, l3_w_gate, l3_w_up, l3_w_down)
    return h.reshape(b, t, d)
```

```python
import functools

import numpy as np
import jax
import jax.numpy as jnp
from jax import lax
from jax.experimental import pallas as pl
from jax.experimental.pallas import tpu as pltpu

D_MODEL = 2048
HEAD_DIM = 128
CONV_CH = D_MODEL // 2
CONV_WIDTH = 31
MOBA_HEADS = (D_MODEL // 2) // HEAD_DIM
MOBA_DIM = MOBA_HEADS * HEAD_DIM
MOBA_BLOCK = 256
MOBA_TOPK = 3
NSA_HEADS = D_MODEL // HEAD_DIM
NSA_KV_HEADS = NSA_HEADS // 4
NSA_GROUP = NSA_HEADS // NSA_KV_HEADS
NSA_DIM = NSA_HEADS * HEAD_DIM
NSA_KV_DIM = NSA_KV_HEADS * HEAD_DIM
CMP_LEN = 32
CMP_STRIDE = 16
CMP_HIDDEN = 256
SLC_BLOCK = 64
SLC_TOPN = 16
WINDOW = 512
D_FF = ((8 * D_MODEL) // 3 + 255) // 256 * 256
N_EXPERTS = 8
TOP_K = 2
EPS = 1e-6
NEG = -1e30

LANES = 128
F32 = jnp.float32
BF16 = jnp.bfloat16
MXU_DTYPE = BF16
HIGHEST = lax.Precision.HIGHEST
VMEM_LIMIT = 56 * 1024 * 1024


def _cparams(sem):
    return pltpu.CompilerParams(dimension_semantics=sem, vmem_limit_bytes=VMEM_LIMIT)


def _rms(x, g):
    return x * lax.rsqrt(jnp.mean(x * x, axis=-1, keepdims=True) + EPS) * g


def _dot(a, b):
    return jnp.dot(a, b, preferred_element_type=F32)


def _dot_nt(a, b, precision=None):
    return lax.dot_general(a, b, (((1,), (1,)), ((), ())), precision=precision,
                           preferred_element_type=F32)


def _sigmoid(x):
    return 1.0 / (1.0 + jnp.exp(-x))


def _norm_matmul_kernel(x_ref, g_ref, w_ref, o_ref, h_ref):
    @pl.when(pl.program_id(1) == 0)
    def _():
        h_ref[...] = _rms(x_ref[...], g_ref[...]).astype(h_ref.dtype)

    o_ref[...] = _dot(h_ref[...], w_ref[...]).astype(o_ref.dtype)


def norm_matmul(x, g, w, *, tm=1024, tn=1024):
    t, d = x.shape
    n = w.shape[1]
    tm = min(tm, t)
    assert t % tm == 0 and n % tn == 0
    return pl.pallas_call(
        _norm_matmul_kernel,
        grid=(t // tm, n // tn),
        in_specs=[pl.BlockSpec((tm, d), lambda i, j: (i, 0)),
                  pl.BlockSpec((1, d), lambda i, j: (0, 0)),
                  pl.BlockSpec((d, tn), lambda i, j: (0, j))],
        out_specs=pl.BlockSpec((tm, tn), lambda i, j: (i, j)),
        out_shape=jax.ShapeDtypeStruct((t, n), F32),
        scratch_shapes=[pltpu.VMEM((tm, d), MXU_DTYPE)],
        compiler_params=_cparams(("arbitrary", "arbitrary")),
        name="norm_matmul",
    )(x, g.reshape(1, d), w)


def _proj_residual_kernel(*refs, n_pairs):
    x_ref = refs[0]
    o_ref = refs[1 + 2 * n_pairs]
    acc = x_ref[...]
    for p in range(n_pairs):
        acc = acc + _dot(refs[1 + 2 * p][...], refs[2 + 2 * p][...])
    o_ref[...] = acc


def proj_residual(x, pairs, *, tm=1024, tn=1024):
    t, d = x.shape
    tm = min(tm, t)
    in_specs = [pl.BlockSpec((tm, tn), lambda i, j: (i, j))]
    args = [x]
    for a, w in pairs:
        k = a.shape[1]
        in_specs.append(pl.BlockSpec((tm, k), lambda i, j: (i, 0)))
        in_specs.append(pl.BlockSpec((k, tn), lambda i, j: (0, j)))
        args += [a, w]
    return pl.pallas_call(
        functools.partial(_proj_residual_kernel, n_pairs=len(pairs)),
        grid=(t // tm, d // tn),
        in_specs=in_specs,
        out_specs=pl.BlockSpec((tm, tn), lambda i, j: (i, j)),
        out_shape=jax.ShapeDtypeStruct((t, d), F32),
        compiler_params=_cparams(("arbitrary", "arbitrary")),
        name="proj_residual",
    )(*args)


CONV_HALO = 32
CONV_ROWS = 32


def _conv_kernel(av_ref, ag_ref, hv_ref, hg_ref, w_ref, b_ref, lg_ref, lb_ref, o_ref, s_ref, *, tq):
    i = pl.program_id(0)
    halo = hv_ref[...] * _sigmoid(hg_ref[...])
    s_ref[0:CONV_HALO, :] = jnp.where(i > 0, halo, 0.0)
    s_ref[CONV_HALO:, :] = av_ref[...] * _sigmoid(ag_ref[...])
    first = CONV_HALO - (CONV_WIDTH - 1)
    for c in range(tq // CONV_ROWS):
        base = c * CONV_ROWS + first
        acc = jnp.broadcast_to(b_ref[...], (CONV_ROWS, CONV_CH))
        for k in range(CONV_WIDTH):
            acc = acc + s_ref[base + k:base + k + CONV_ROWS, :] * w_ref[k:k + 1, :]
        mu = jnp.mean(acc, axis=-1, keepdims=True)
        cen = acc - mu
        var = jnp.mean(cen * cen, axis=-1, keepdims=True)
        y = cen * lax.rsqrt(var + EPS) * lg_ref[...] + lb_ref[...]
        o_ref[c * CONV_ROWS:(c + 1) * CONV_ROWS, :] = (y * _sigmoid(y)).astype(o_ref.dtype)


def conv_module(u, conv_w, conv_b, ln_g, ln_b, *, tq=512):
    t = u.shape[0]
    tq = min(tq, t)
    c = CONV_CH
    hb = tq // CONV_HALO
    w = jnp.pad(conv_w.reshape(CONV_WIDTH, c), ((0, 1), (0, 0)))
    row = lambda v: v.reshape(1, c)
    const = lambda i: (0, 0)
    return pl.pallas_call(
        functools.partial(_conv_kernel, tq=tq),
        grid=(t // tq,),
        in_specs=[pl.BlockSpec((tq, c), lambda i: (i, 0)),
                  pl.BlockSpec((tq, c), lambda i: (i, 1)),
                  pl.BlockSpec((CONV_HALO, c), lambda i: (jnp.maximum(i * hb - 1, 0), 0)),
                  pl.BlockSpec((CONV_HALO, c), lambda i: (jnp.maximum(i * hb - 1, 0), 1)),
                  pl.BlockSpec((CONV_WIDTH + 1, c), const),
                  pl.BlockSpec((1, c), const), pl.BlockSpec((1, c), const), pl.BlockSpec((1, c), const)],
        out_specs=pl.BlockSpec((tq, c), lambda i: (i, 0)),
        out_shape=jax.ShapeDtypeStruct((t, c), MXU_DTYPE),
        scratch_shapes=[pltpu.VMEM((tq + CONV_HALO, c), F32)],
        compiler_params=_cparams(("arbitrary",)),
        name="conv_module",
    )(u, u, u, u, w, row(conv_b), row(ln_g), row(ln_b))


ONES_ROWS = 16
LOG2E = 1.4426950408889634
QK_SCALE_LOG2 = HEAD_DIM ** -0.5 * LOG2E


def _kv_prologue(k_ref, v_ref, kg_ref, kaug_s, vt_s, *, t, tk, blk):
    kn = _rms(k_ref[...], kg_ref[...])
    kaug_s[:, 0:HEAD_DIM] = kn.astype(kaug_s.dtype)
    if blk is not None:
        lane_blk = lax.broadcasted_iota(jnp.int32, (t, LANES), 1)
        key_blk = lax.broadcasted_iota(jnp.int32, (t, LANES), 0) // blk
        kaug_s[:, HEAD_DIM:] = jnp.where(lane_blk == key_blk, 1.0, 0.0).astype(kaug_s.dtype)
    for c in range(t // tk):
        vt_s[c, 0:HEAD_DIM, :] = v_ref[c * tk:(c + 1) * tk, :].T.astype(vt_s.dtype)
        vt_s[c, HEAD_DIM:, :] = jnp.ones((ONES_ROWS, tk), vt_s.dtype)
    return kn


PLAIN, CAUSAL, WINDOW_TAIL = 0, 1, 2


def _tile_pattern(kind, tk, tq):
    key_i = lax.broadcasted_iota(jnp.int32, (tk, tq), 0)
    qry_i = lax.broadcasted_iota(jnp.int32, (tk, tq), 1)
    if kind == CAUSAL:
        return key_i <= qry_i
    if kind == WINDOW_TAIL:
        return key_i > qry_i
    return None


def _fill_bias(sb_s, slopes, kinds, *, tk, tq):
    key_off = lax.broadcasted_iota(jnp.int32, (tk, tq), 0).astype(F32)
    for r, slope in enumerate(slopes):
        bias = (LOG2E * slope) * key_off
        for v, kind in enumerate(kinds):
            pattern = _tile_pattern(kind, tk, tq)
            sb_s[v, r] = bias if pattern is None else jnp.where(pattern, bias, NEG)


def _flash_init(m_s, acc_s):
    m_s[...] = jnp.full_like(m_s, NEG)
    acc_s[...] = jnp.zeros_like(acc_s)


def _qk_tile(j, slot, *, heads, qaug_s, kaug_s, s_s, tq, tk):
    kj = kaug_s[pl.ds(pl.multiple_of(j * tk, tk), tk), :]
    for r in range(heads):
        s_s[slot, r] = _dot_nt(kj, qaug_s[r * tq:(r + 1) * tq, :])


def _softmax_pv_tile(j, slot, variant, *, t0, slopes, vt_s, s_s, sb_s, m_s, acc_s, tq, tk):
    vtj = vt_s[j]
    off = (j * tk - t0).astype(F32)
    for r, slope in enumerate(slopes):
        cols = slice(r * tq, (r + 1) * tq)
        s_r = s_s[slot, r] + sb_s[variant, r]
        c = (LOG2E * slope) * off
        m_old = m_s[:, cols]
        m_new = jnp.maximum(m_old, jnp.max(s_r, axis=0, keepdims=True) + c)
        p_r = jnp.exp2(s_r - (m_new - c)).astype(MXU_DTYPE)
        acc_s[:, cols] = jnp.exp2(m_old - m_new) * acc_s[:, cols] + _dot(vtj, p_r)
        m_s[:, cols] = m_new


SWEEP_UNROLL = 4


def _causal_sweep(qk, spv, tile_at, n_past):
    qk(tile_at(0), 0)

    def visit(k, count):
        for u in range(count):
            qk(tile_at(k + u + 1), (u + 1) % 2)
            spv(tile_at(k + u), u % 2, PLAIN)

    def trip(i, carry):
        visit(SWEEP_UNROLL * i, SWEEP_UNROLL)
        return carry

    lax.fori_loop(0, n_past // SWEEP_UNROLL, trip, 0)
    k = (n_past // SWEEP_UNROLL) * SWEEP_UNROLL
    for rest in range(SWEEP_UNROLL):
        @pl.when(n_past % SWEEP_UNROLL == rest)
        def _():
            visit(k, rest)
            spv(tile_at(k + rest), rest % 2, CAUSAL)


def _flash_result(acc_s, r, tq):
    cols = slice(r * tq, (r + 1) * tq)
    o_t = acc_s[0:HEAD_DIM, cols] / acc_s[HEAD_DIM:HEAD_DIM + 1, cols]
    return o_t.T


def _flash_scratch(t, tq, tk, heads, kaug_cols, n_patterns, n_slots):
    nq = heads * tq
    return [pltpu.VMEM((t, kaug_cols), MXU_DTYPE),
            pltpu.VMEM((t // tk, HEAD_DIM + ONES_ROWS, tk), MXU_DTYPE),
            pltpu.VMEM((nq, kaug_cols), MXU_DTYPE),
            pltpu.VMEM((n_patterns, heads, tk, tq), F32),
            pltpu.VMEM((n_slots, heads, tk, tq), F32),
            pltpu.VMEM((1, nq), F32),
            pltpu.VMEM((HEAD_DIM + ONES_ROWS, nq), F32)]


def _flash_stages(*, t0, slopes, qaug_s, kaug_s, vt_s, sb_s, s_s, m_s, acc_s, tq, tk):
    qk = functools.partial(_qk_tile, heads=len(slopes), qaug_s=qaug_s, kaug_s=kaug_s, s_s=s_s, tq=tq, tk=tk)
    spv = functools.partial(_softmax_pv_tile, t0=t0, slopes=slopes, vt_s=vt_s, s_s=s_s, sb_s=sb_s, m_s=m_s,
                            acc_s=acc_s, tq=tq, tk=tk)
    return qk, spv


def _moba_kernel(slopes_ref, q_ref, k_ref, v_ref, qg_ref, kg_ref, o_ref,
                 kaug_s, vt_s, qaug_s, sb_s, s_s, m_s, acc_s, km_s, *, tq, t_total):
    h = pl.program_id(0)
    qi = pl.program_id(1)
    nb = t_total // MOBA_BLOCK
    tk = tq

    slopes = [slopes_ref[h]]

    @pl.when(qi == 0)
    def _():
        kn = _kv_prologue(k_ref, v_ref, kg_ref, kaug_s, vt_s, t=t_total, tk=tk, blk=MOBA_BLOCK)
        km_s[...] = jnp.zeros_like(km_s)
        km_s[0:nb, :] = jnp.mean(kn.reshape(nb, MOBA_BLOCK, HEAD_DIM), axis=1)
        _fill_bias(sb_s, slopes, (PLAIN, CAUSAL), tk=tk, tq=tq)

    qn = _rms(q_ref[...], qg_ref[...])
    nb8 = -(-nb // 8) * 8
    gate = _dot_nt(km_s[0:nb8, :], qn, precision=HIGHEST)
    blk = lax.broadcasted_iota(jnp.int32, (nb8, tq), 0)
    blk_f = blk.astype(F32)
    qb = (qi * tq + lax.broadcasted_iota(jnp.int32, (nb8, tq), 1)) // MOBA_BLOCK
    past = blk < qb
    gate = jnp.where(past, gate, -jnp.inf)
    sel = blk == qb
    for _ in range(MOBA_TOPK):
        top = jnp.max(gate, axis=0, keepdims=True)
        first = jnp.min(jnp.where(gate == top, blk_f, float(nb8)), axis=0, keepdims=True)
        pick = blk_f == first
        sel = sel | (pick & past)
        gate = jnp.where(pick, -jnp.inf, gate)
    sel_bias = jnp.where(sel, 0.0, NEG)
    if nb8 < LANES:
        sel_bias = jnp.concatenate([sel_bias, jnp.full((LANES - nb8, tq), NEG, F32)], axis=0)
    qaug_s[:, 0:HEAD_DIM] = (qn * QK_SCALE_LOG2).astype(qaug_s.dtype)
    qaug_s[:, HEAD_DIM:] = sel_bias.T.astype(qaug_s.dtype)

    _flash_init(m_s, acc_s)
    qk, spv = _flash_stages(t0=qi * tq, slopes=slopes, qaug_s=qaug_s, kaug_s=kaug_s, vt_s=vt_s, sb_s=sb_s,
                            s_s=s_s, m_s=m_s, acc_s=acc_s, tq=tq, tk=tk)
    _causal_sweep(qk, spv, lambda k: k, qi)
    o_ref[...] = _flash_result(acc_s, 0, tq).astype(o_ref.dtype)


def moba_attention(u, q_norm, k_norm, *, col0, tq=512):
    t = u.shape[0]
    tq = min(tq, t)
    assert t // MOBA_BLOCK <= LANES and tq % MOBA_BLOCK == 0
    nh = MOBA_HEADS
    slopes = jnp.asarray(2.0 ** (-8.0 * np.arange(1, nh + 1) / nh), dtype=F32)
    grid_spec = pltpu.PrefetchScalarGridSpec(
        num_scalar_prefetch=1,
        grid=(nh, t // tq),
        in_specs=[pl.BlockSpec((tq, HEAD_DIM), lambda h, i, s: (i, col0 + h)),
                  pl.BlockSpec((t, HEAD_DIM), lambda h, i, s: (0, col0 + nh + h)),
                  pl.BlockSpec((t, HEAD_DIM), lambda h, i, s: (0, col0 + 2 * nh + h)),
                  pl.BlockSpec((1, HEAD_DIM), lambda h, i, s: (0, 0)),
                  pl.BlockSpec((1, HEAD_DIM), lambda h, i, s: (0, 0))],
        out_specs=pl.BlockSpec((tq, HEAD_DIM), lambda h, i, s: (i, h)),
        scratch_shapes=_flash_scratch(t, tq, tq, 1, HEAD_DIM + LANES, 2, 2) + [pltpu.VMEM((LANES, HEAD_DIM), F32)])
    return pl.pallas_call(
        functools.partial(_moba_kernel, tq=tq, t_total=t),
        grid_spec=grid_spec,
        out_shape=jax.ShapeDtypeStruct((t, MOBA_DIM), MXU_DTYPE),
        compiler_params=_cparams(("arbitrary", "arbitrary")),
        name="moba_attention",
    )(slopes, u, u, u, q_norm.reshape(1, HEAD_DIM), k_norm.reshape(1, HEAD_DIM))


def _swiglu_step(h, wg_ref, wu_ref, wd_ref):
    a = _dot(h, wg_ref[...])
    b = _dot(h, wu_ref[...])
    mid = (a * _sigmoid(a) * b).astype(MXU_DTYPE)
    return _dot(mid, wd_ref[...])


def _dense_ffn_kernel(x_ref, g_ref, wg_ref, wu_ref, wd_ref, o_ref, h_ref):
    @pl.when(pl.program_id(1) == 0)
    def _():
        x = x_ref[...]
        h_ref[...] = _rms(x, g_ref[...]).astype(h_ref.dtype)
        o_ref[...] = x

    o_ref[...] += _swiglu_step(h_ref[...], wg_ref, wu_ref, wd_ref)


def dense_ffn(x, g, wg, wu, wd, *, tm=512, tf=512):
    t, d = x.shape
    f = wg.shape[1]
    tm = min(tm, t)
    return pl.pallas_call(
        _dense_ffn_kernel,
        grid=(t // tm, f // tf),
        in_specs=[pl.BlockSpec((tm, d), lambda i, j: (i, 0)),
                  pl.BlockSpec((1, d), lambda i, j: (0, 0)),
                  pl.BlockSpec((d, tf), lambda i, j: (0, j)),
                  pl.BlockSpec((d, tf), lambda i, j: (0, j)),
                  pl.BlockSpec((tf, d), lambda i, j: (j, 0))],
        out_specs=pl.BlockSpec((tm, d), lambda i, j: (i, 0)),
        out_shape=jax.ShapeDtypeStruct((t, d), F32),
        scratch_shapes=[pltpu.VMEM((tm, d), MXU_DTYPE)],
        compiler_params=_cparams(("arbitrary", "arbitrary")),
        name="dense_ffn",
    )(x, g.reshape(1, d), wg, wu, wd)


def _grouped_ffn_kernel(te_ref, rows_ref, x_ref, wg_ref, wu_ref, wd_ref, o_ref, h_ref, *, tm):
    rows = rows_ref[pl.program_id(0)]
    half = tm // 2

    @pl.when(pl.program_id(1) == 0)
    def _():
        h_ref[...] = x_ref[...].astype(h_ref.dtype)
        o_ref[...] = jnp.zeros_like(o_ref)

    @pl.when(rows > half)
    def _():
        o_ref[...] += _swiglu_step(h_ref[...], wg_ref, wu_ref, wd_ref)

    @pl.when((rows > 0) & (rows <= half))
    def _():
        o_ref[0:half, :] += _swiglu_step(h_ref[0:half, :], wg_ref, wu_ref, wd_ref)


def grouped_ffn(xs, tile_expert, tile_rows, wg, wu, wd, *, tm, tf=512):
    m, d = xs.shape
    f = wg.shape[2]
    nf = f // tf

    def fcol(i, j, rows):
        return jnp.where(rows[i] > 0, j, nf - 1)

    grid_spec = pltpu.PrefetchScalarGridSpec(
        num_scalar_prefetch=2,
        grid=(m // tm, nf),
        in_specs=[pl.BlockSpec((tm, d), lambda i, j, te, rows: (i, 0)),
                  pl.BlockSpec((None, d, tf), lambda i, j, te, rows: (te[i], 0, fcol(i, j, rows))),
                  pl.BlockSpec((None, d, tf), lambda i, j, te, rows: (te[i], 0, fcol(i, j, rows))),
                  pl.BlockSpec((None, tf, d), lambda i, j, te, rows: (te[i], fcol(i, j, rows), 0))],
        out_specs=pl.BlockSpec((tm, d), lambda i, j, te, rows: (i, 0)),
        scratch_shapes=[pltpu.VMEM((tm, d), MXU_DTYPE)])
    return pl.pallas_call(
        functools.partial(_grouped_ffn_kernel, tm=tm),
        grid_spec=grid_spec,
        out_shape=jax.ShapeDtypeStruct((m, d), F32),
        compiler_params=_cparams(("arbitrary", "arbitrary")),
        name="grouped_ffn",
    )(tile_expert, tile_rows, xs, wg, wu, wd)


META_E, META_W, META_RANK = 0, 2, 4


def _router_kernel(x_ref, g_ref, wr_ref, h_ref, meta_ref, cnt_ref, carry_s, *, tm):
    @pl.when(pl.program_id(0) == 0)
    def _():
        carry_s[...] = jnp.zeros_like(carry_s)

    h = _rms(x_ref[...], g_ref[...])
    h_ref[...] = h
    logits = jnp.dot(h, wr_ref[...], precision=HIGHEST, preferred_element_type=F32)
    lane = lax.broadcasted_iota(jnp.int32, (tm, LANES), 1).astype(F32)
    lg = jnp.where(lane < N_EXPERTS, logits, -jnp.inf)

    def take_top(v):
        top = jnp.max(v, axis=-1, keepdims=True)
        idx = jnp.min(jnp.where(v == top, lane, float(LANES)), axis=-1, keepdims=True)
        return top, idx

    m1, i1 = take_top(lg)
    m2, i2 = take_top(jnp.where(lane == i1, -jnp.inf, lg))
    e = jnp.exp(m2 - m1)
    w1 = 1.0 / (1.0 + e)
    w2 = e / (1.0 + e)
    oh1 = lane == i1
    oh2 = lane == i2
    oh = jnp.where(oh1 | oh2, 1.0, 0.0)
    r = lax.broadcasted_iota(jnp.int32, (tm, tm), 0)
    c = lax.broadcasted_iota(jnp.int32, (tm, tm), 1)
    lower = jnp.where(c < r, 1.0, 0.0).astype(BF16)
    before = _dot(lower, oh.astype(BF16)) + carry_s[...]
    rank1 = jnp.sum(jnp.where(oh1, before, 0.0), axis=-1, keepdims=True)
    rank2 = jnp.sum(jnp.where(oh2, before, 0.0), axis=-1, keepdims=True)
    carry_s[...] += jnp.sum(oh, axis=0, keepdims=True)
    meta = jnp.zeros((tm, LANES), F32)
    for k, v in ((META_E, i1), (META_E + 1, i2), (META_W, w1), (META_W + 1, w2),
                 (META_RANK, rank1), (META_RANK + 1, rank2)):
        meta = jnp.where(lane == k, v, meta)
    meta_ref[...] = meta
    cnt_ref[...] = jnp.broadcast_to(carry_s[...], cnt_ref.shape)


def moe_router(x, g, w_router, *, tm=512):
    t, d = x.shape
    tm = min(tm, t)
    wr = jnp.pad(w_router, ((0, 0), (0, LANES - N_EXPERTS)))
    return pl.pallas_call(
        functools.partial(_router_kernel, tm=tm),
        grid=(t // tm,),
        in_specs=[pl.BlockSpec((tm, d), lambda i: (i, 0)),
                  pl.BlockSpec((1, d), lambda i: (0, 0)),
                  pl.BlockSpec((d, LANES), lambda i: (0, 0))],
        out_specs=[pl.BlockSpec((tm, d), lambda i: (i, 0)),
                   pl.BlockSpec((tm, LANES), lambda i: (i, 0)),
                   pl.BlockSpec((8, LANES), lambda i: (0, 0))],
        out_shape=[jax.ShapeDtypeStruct((t, d), F32),
                   jax.ShapeDtypeStruct((t, LANES), F32),
                   jax.ShapeDtypeStruct((8, LANES), F32)],
        scratch_shapes=[pltpu.VMEM((1, LANES), F32)],
        compiler_params=_cparams(("arbitrary",)),
        name="moe_router",
    )(x, g.reshape(1, d), wr)


def _dispatch_kernel(p1_ref, p2_ref, h_ref, init_ref, xs_ref, sem, *, tr):
    del init_ref
    base = pl.program_id(0) * tr

    def copies(r, tok):
        src = h_ref.at[pl.ds(r, 1)]
        return (pltpu.make_async_copy(src, xs_ref.at[pl.ds(p1_ref[tok], 1)], sem.at[0]),
                pltpu.make_async_copy(src, xs_ref.at[pl.ds(p2_ref[tok], 1)], sem.at[1]))

    def issue(r, carry):
        for c in copies(r, base + r):
            c.start()
        return carry

    lax.fori_loop(0, tr, issue, 0)

    def drain(r, carry):
        for c in copies(0, 0):
            c.wait()
        return carry

    lax.fori_loop(0, tr, drain, 0)


def moe_dispatch(h, pos1, pos2, m_pad, *, tr=256):
    t, d = h.shape
    tr = min(tr, t)
    grid_spec = pltpu.PrefetchScalarGridSpec(
        num_scalar_prefetch=2,
        grid=(t // tr,),
        in_specs=[pl.BlockSpec((tr, d), lambda i, p1, p2: (i, 0)),
                  pl.BlockSpec(memory_space=pl.ANY)],
        out_specs=pl.BlockSpec(memory_space=pl.ANY),
        scratch_shapes=[pltpu.SemaphoreType.DMA((2,))])
    return pl.pallas_call(
        functools.partial(_dispatch_kernel, tr=tr),
        grid_spec=grid_spec,
        out_shape=jax.ShapeDtypeStruct((m_pad, d), h.dtype),
        input_output_aliases={3: 0},
        compiler_params=_cparams(("arbitrary",)),
        name="moe_dispatch",
    )(pos1, pos2, h, jnp.zeros((m_pad, d), h.dtype))


def _combine_kernel(p1_ref, p2_ref, x_ref, meta_ref, y_ref, o_ref, y1_s, y2_s, sem, *, tr):
    base = pl.program_id(0) * tr

    def copies(r, tok):
        return (pltpu.make_async_copy(y_ref.at[pl.ds(p1_ref[tok], 1)], y1_s.at[pl.ds(r, 1)], sem.at[0]),
                pltpu.make_async_copy(y_ref.at[pl.ds(p2_ref[tok], 1)], y2_s.at[pl.ds(r, 1)], sem.at[1]))

    def issue(r, carry):
        for c in copies(r, base + r):
            c.start()
        return carry

    lax.fori_loop(0, tr, issue, 0)

    def drain(r, carry):
        for c in copies(0, 0):
            c.wait()
        return carry

    lax.fori_loop(0, tr, drain, 0)
    meta = meta_ref[...]
    w1 = meta[:, META_W:META_W + 1]
    w2 = meta[:, META_W + 1:META_W + 2]
    o_ref[...] = x_ref[...] + w1 * y1_s[...] + w2 * y2_s[...]


def moe_combine(x, meta, y, pos1, pos2, *, tr=256):
    t, d = x.shape
    tr = min(tr, t)
    grid_spec = pltpu.PrefetchScalarGridSpec(
        num_scalar_prefetch=2,
        grid=(t // tr,),
        in_specs=[pl.BlockSpec((tr, d), lambda i, p1, p2: (i, 0)),
                  pl.BlockSpec((tr, LANES), lambda i, p1, p2: (i, 0)),
                  pl.BlockSpec(memory_space=pl.ANY)],
        out_specs=pl.BlockSpec((tr, d), lambda i, p1, p2: (i, 0)),
        scratch_shapes=[pltpu.VMEM((tr, d), F32), pltpu.VMEM((tr, d), F32),
                        pltpu.SemaphoreType.DMA((2,))])
    return pl.pallas_call(
        functools.partial(_combine_kernel, tr=tr),
        grid_spec=grid_spec,
        out_shape=jax.ShapeDtypeStruct((t, d), F32),
        compiler_params=_cparams(("arbitrary",)),
        name="moe_combine",
    )(pos1, pos2, x, meta, y)


def moe_ffn(x, ffn_norm, w_router, wg, wu, wd, *, tm=512):
    t, d = x.shape
    tm = min(tm, t)
    h, meta, cnt = moe_router(x, ffn_norm, w_router)
    counts = cnt[0, :N_EXPERTS].astype(jnp.int32)
    padded = (counts + tm - 1) // tm * tm
    ends = jnp.cumsum(padded)
    starts = ends - padded
    ids = meta[:, META_E:META_E + 2].astype(jnp.int32)
    ranks = meta[:, META_RANK:META_RANK + 2].astype(jnp.int32)
    pos = starts[ids] + ranks
    pos1, pos2 = pos[:, 0], pos[:, 1]
    m_pad = TOP_K * t + N_EXPERTS * tm
    n_tiles = m_pad // tm
    tile_start = jnp.arange(n_tiles, dtype=jnp.int32) * tm
    tile_expert = jnp.sum((tile_start[:, None] >= ends[None, :]).astype(jnp.int32), axis=1)
    last_expert = jnp.sum((ends[-1] - 1 >= ends).astype(jnp.int32))
    used = tile_start < ends[-1]
    tile_expert = jnp.where(used, tile_expert, last_expert).astype(jnp.int32)
    tile_rows = jnp.clip((starts + counts)[tile_expert] - tile_start, 0, tm)
    tile_rows = jnp.where(used, tile_rows, 0).astype(jnp.int32)
    xs = moe_dispatch(h, pos1, pos2, m_pad)
    y = grouped_ffn(xs, tile_expert, tile_rows, wg, wu, wd, tm=tm)
    return moe_combine(x, meta, y, pos1, pos2)


def _compress_kernel(c_ref, pe_ref, w1_ref, w2_ref, ng_ref, o_ref, *, nrows, half, normalize):
    c = c_ref[...].astype(MXU_DTYPE)
    top = _dot(c, w1_ref[0:half, :])
    bot = _dot(c, w1_ref[half:, :])
    bot_next = pltpu.roll(bot, nrows - 1, 0)
    pe_term = _dot(pe_ref[...].astype(MXU_DTYPE), w1_ref[...])
    hid = top + bot_next + pe_term
    out = _dot((hid * _sigmoid(hid)).astype(MXU_DTYPE), w2_ref[...])
    if normalize:
        out = _rms(out, ng_ref[...])
    valid = lax.broadcasted_iota(jnp.int32, (nrows, 1), 0) < nrows - 1
    o_ref[...] = jnp.where(valid, out, 0.0)


def compress(c, pe, w1, w2, norm_g, *, normalize):
    g, nrows, half = c.shape
    return pl.pallas_call(
        functools.partial(_compress_kernel, nrows=nrows, half=half, normalize=normalize),
        grid=(g,),
        in_specs=[pl.BlockSpec((None, nrows, half), lambda i: (i, 0, 0)),
                  pl.BlockSpec((1, 2 * half), lambda i: (0, 0)),
                  pl.BlockSpec((2 * half, CMP_HIDDEN), lambda i: (0, 0)),
                  pl.BlockSpec((CMP_HIDDEN, HEAD_DIM), lambda i: (0, 0)),
                  pl.BlockSpec((1, HEAD_DIM), lambda i: (0, 0))],
        out_specs=pl.BlockSpec((None, nrows, HEAD_DIM), lambda i: (i, 0, 0)),
        out_shape=jax.ShapeDtypeStruct((g, nrows, HEAD_DIM), F32),
        compiler_params=_cparams(("arbitrary",)),
        name="nsa_compress",
    )(c, pe.reshape(1, 2 * half), w1, w2, norm_g.reshape(1, HEAD_DIM))


def _nsa_cmp_kernel(slopes_ref, q_ref, kc_ref, vc_ref, qg_ref, ov_ref, oc_ref, sel_ref, used_ref, kc_s, vct_s,
                    psum_s, *, tq, ncmp, nsel):
    g = pl.program_id(0)
    qi = pl.program_id(1)

    @pl.when(qi == 0)
    def _():
        kc_s[...] = kc_ref[...].astype(kc_s.dtype)
        vct_s[...] = vc_ref[...].T.astype(vct_s.dtype)

    def heads(nk):
        tpos = qi * tq + lax.broadcasted_iota(jnp.int32, (nk, tq), 1)
        cend = lax.broadcasted_iota(jnp.int32, (nk, tq), 0) * CMP_STRIDE + (CMP_LEN - 1)
        dist = tpos - cend
        ok = dist >= 0
        dist_f = dist.astype(F32)
        any_ok = jnp.where(tpos[0:1, :] >= CMP_LEN - 1, 1.0, 0.0)
        psum = jnp.zeros((nk, tq), F32)
        for r in range(NSA_GROUP):
            cols = slice(r * HEAD_DIM, (r + 1) * HEAD_DIM)
            qn = _rms(q_ref[:, cols], qg_ref[...])
            s = _dot_nt(kc_s[0:nk, :], (qn * HEAD_DIM ** -0.5).astype(MXU_DTYPE))
            s = jnp.where(ok, s - slopes_ref[g * NSA_GROUP + r] * dist_f, NEG)
            p = jnp.exp(s - jnp.max(s, axis=0, keepdims=True))
            p = p * (any_ok / jnp.sum(p, axis=0, keepdims=True))
            oc_ref[:, cols] = _dot(vct_s[:, 0:nk], p.astype(MXU_DTYPE)).T.astype(oc_ref.dtype)
            psum = psum + p
        psum_s[0:nk, :] = psum
        if nk < ncmp:
            psum_s[nk:, :] = jnp.zeros((ncmp - nk, tq), F32)

    n_prefix = ncmp // LANES if ncmp % LANES == 0 else 1
    step_keys = ncmp // n_prefix
    n_visible = (qi * tq + tq - CMP_LEN) // CMP_STRIDE + 1
    need = jnp.minimum((n_visible + step_keys - 1) // step_keys, n_prefix)
    for v in range(1, n_prefix + 1):
        @pl.when(need == v)
        def _():
            heads(v * step_keys)

    psum = psum_s[...]
    p_hi = psum.astype(MXU_DTYPE)
    p_lo = (psum - p_hi.astype(F32)).astype(MXU_DTYPE)
    imp = _dot(ov_ref[...], p_hi) + _dot(ov_ref[...], p_lo)
    blk = lax.broadcasted_iota(jnp.int32, (LANES, tq), 0)
    qb = (qi * tq + lax.broadcasted_iota(jnp.int32, (LANES, tq), 1)) // SLC_BLOCK
    forced = (blk == 0) | (blk == qb) | (blk == qb - 1)
    visible = blk <= qb
    imp = jnp.where(visible & jnp.logical_not(forced), imp, -jnp.inf)
    blk_f = blk.astype(F32)
    sel = forced
    for _ in range(min(SLC_TOPN, nsel) - 3):
        top = jnp.max(imp, axis=0, keepdims=True)
        first = jnp.min(jnp.where(imp == top, blk_f, float(LANES)), axis=0, keepdims=True)
        pick = blk_f == first
        sel = sel | pick
        imp = jnp.where(pick, -jnp.inf, imp)
    sel_q = jnp.where(sel & visible, 1.0, 0.0).T
    sel_ref[...] = sel_q.astype(sel_ref.dtype)
    used_ref[...] = jnp.broadcast_to(jnp.max(sel_q, axis=0, keepdims=True), used_ref.shape)


def nsa_compressed(u, k_cmp, v_cmp, q_norm, *, tq=256):
    t = u.shape[0]
    tq = min(tq, t)
    ncmp = k_cmp.shape[1]
    nsel = t // SLC_BLOCK
    assert nsel <= LANES
    nstart = np.arange(ncmp)[:, None] * CMP_STRIDE
    lo = np.arange(LANES)[None, :] * SLC_BLOCK
    overlap = ((nstart < lo + SLC_BLOCK) & (nstart + CMP_LEN > lo) & (np.arange(ncmp)[:, None] < ncmp - 1))
    ov_t = jnp.asarray(overlap.T.astype(np.float32)).astype(MXU_DTYPE)
    gw = NSA_GROUP * HEAD_DIM
    grid_spec = pltpu.PrefetchScalarGridSpec(
        num_scalar_prefetch=1,
        grid=(NSA_KV_HEADS, t // tq),
        in_specs=[pl.BlockSpec((tq, gw), lambda g, i, s: (i, g)),
                  pl.BlockSpec((None, ncmp, HEAD_DIM), lambda g, i, s: (g, 0, 0)),
                  pl.BlockSpec((None, ncmp, HEAD_DIM), lambda g, i, s: (g, 0, 0)),
                  pl.BlockSpec((1, HEAD_DIM), lambda g, i, s: (0, 0)),
                  pl.BlockSpec((LANES, ncmp), lambda g, i, s: (0, 0))],
        out_specs=[pl.BlockSpec((tq, gw), lambda g, i, s: (i, g)),
                   pl.BlockSpec((None, tq, LANES), lambda g, i, s: (g, i, 0)),
                   pl.BlockSpec((None, None, 8, LANES), lambda g, i, s: (g, i, 0, 0))],
        scratch_shapes=[pltpu.VMEM((ncmp, HEAD_DIM), MXU_DTYPE), pltpu.VMEM((HEAD_DIM, ncmp), MXU_DTYPE),
                        pltpu.VMEM((ncmp, tq), F32)])
    return pl.pallas_call(
        functools.partial(_nsa_cmp_kernel, tq=tq, ncmp=ncmp, nsel=nsel),
        grid_spec=grid_spec,
        out_shape=[jax.ShapeDtypeStruct((t, NSA_DIM), F32),
                   jax.ShapeDtypeStruct((NSA_KV_HEADS, t, LANES), MXU_DTYPE),
                   jax.ShapeDtypeStruct((NSA_KV_HEADS, t // tq, 8, LANES), F32)],
        compiler_params=_cparams(("arbitrary", "arbitrary")),
        name="nsa_compressed",
    )(_nsa_slopes(), u, k_cmp, v_cmp, q_norm.reshape(1, HEAD_DIM), ov_t)


def _nsa_slopes():
    return jnp.asarray(2.0 ** (-8.0 * np.arange(1, NSA_HEADS + 1) / NSA_HEADS), dtype=F32)


def _group_queries(q_ref, qg_ref, qaug_s, tq):
    for r in range(NSA_GROUP):
        qn = _rms(q_ref[:, r * HEAD_DIM:(r + 1) * HEAD_DIM], qg_ref[...])
        qaug_s[r * tq:(r + 1) * tq, 0:HEAD_DIM] = (qn * QK_SCALE_LOG2).astype(qaug_s.dtype)


def _nsa_slc_kernel(slopes_ref, tiles_ref, npast_ref, q_ref, k_ref, v_ref, sel_ref, qg_ref, kg_ref, o_ref,
                    kaug_s, vt_s, qaug_s, sb_s, s_s, m_s, acc_s, *, tq, t_total):
    g = pl.program_id(0)
    qi = pl.program_id(1)
    tk = tq
    row_len = t_total // tk + 1
    slopes = [slopes_ref[g * NSA_GROUP + r] for r in range(NSA_GROUP)]

    @pl.when(qi == 0)
    def _():
        _kv_prologue(k_ref, v_ref, kg_ref, kaug_s, vt_s, t=t_total, tk=tk, blk=SLC_BLOCK)
        _fill_bias(sb_s, slopes, (PLAIN, CAUSAL), tk=tk, tq=tq)

    _group_queries(q_ref, qg_ref, qaug_s, tq)
    sel_bias = jnp.where(sel_ref[...].astype(F32) > 0.5, 0.0, NEG).astype(qaug_s.dtype)
    for r in range(NSA_GROUP):
        qaug_s[r * tq:(r + 1) * tq, HEAD_DIM:] = sel_bias
    _flash_init(m_s, acc_s)
    qk, spv = _flash_stages(t0=qi * tq, slopes=slopes, qaug_s=qaug_s, kaug_s=kaug_s, vt_s=vt_s, sb_s=sb_s,
                            s_s=s_s, m_s=m_s, acc_s=acc_s, tq=tq, tk=tk)
    row = g * pl.num_programs(1) + qi
    _causal_sweep(qk, spv, lambda k: tiles_ref[row * row_len + k], npast_ref[row])
    for r in range(NSA_GROUP):
        o_ref[:, r * HEAD_DIM:(r + 1) * HEAD_DIM] = _flash_result(acc_s, r, tq).astype(o_ref.dtype)


def _nsa_win_kernel(slopes_ref, q_ref, k_ref, v_ref, oc_ref, os_ref, gl_ref, qg_ref, kg_ref, o_ref,
                    kaug_s, vt_s, qaug_s, sb_s, s_s, m_s, acc_s, *, tq, t_total):
    g = pl.program_id(0)
    qi = pl.program_id(1)
    tk = tq
    nw = WINDOW // tk
    slopes = [slopes_ref[g * NSA_GROUP + r] for r in range(NSA_GROUP)]

    @pl.when(qi == 0)
    def _():
        _kv_prologue(k_ref, v_ref, kg_ref, kaug_s, vt_s, t=t_total, tk=tk, blk=None)
        _fill_bias(sb_s, slopes, (PLAIN, CAUSAL, WINDOW_TAIL), tk=tk, tq=tq)

    _group_queries(q_ref, qg_ref, qaug_s, tq)
    _flash_init(m_s, acc_s)
    qk, spv = _flash_stages(t0=qi * tq, slopes=slopes, qaug_s=qaug_s, kaug_s=kaug_s, vt_s=vt_s, sb_s=sb_s,
                            s_s=s_s, m_s=m_s, acc_s=acc_s, tq=tq, tk=tk)
    kinds = [WINDOW_TAIL] + [PLAIN] * (nw - 1) + [CAUSAL]

    @pl.when(qi >= nw)
    def _():
        for i in range(nw + 1):
            qk(qi - nw + i, i)
        for i, kind in enumerate(kinds):
            spv(qi - nw + i, i, kind)

    @pl.when(qi < nw)
    def _():
        for back in range(nw - 1, -1, -1):
            @pl.when(qi >= back)
            def _():
                qk(qi - back, 0)
                spv(qi - back, 0, kinds[nw - back])
    gates = _sigmoid(gl_ref[...])
    for r in range(NSA_GROUP):
        cols = slice(r * HEAD_DIM, (r + 1) * HEAD_DIM)
        lane0 = 3 * r
        o = (gates[:, lane0:lane0 + 1] * oc_ref[:, cols] + gates[:, lane0 + 1:lane0 + 2] * os_ref[:, cols]
             + gates[:, lane0 + 2:lane0 + 3] * _flash_result(acc_s, r, tq))
        o_ref[:, cols] = o.astype(o_ref.dtype)


def nsa_selected(u, sel, blocks_used, q_norm, k_norm, *, kcol, vcol, tq=256):
    t = u.shape[0]
    tq = min(tq, t)
    gw = NSA_GROUP * HEAD_DIM
    per_tile = tq // SLC_BLOCK
    nt = t // tq
    used = blocks_used[:, :, 0, :t // SLC_BLOCK].reshape(NSA_KV_HEADS, nt, nt, per_tile)
    tile_ids = jnp.arange(nt, dtype=jnp.int32)
    past_used = (jnp.max(used, axis=-1) > 0) & (tile_ids[None, None, :] < tile_ids[None, :, None])
    n_past = jnp.sum(past_used, axis=-1).astype(jnp.int32)
    order = jnp.argsort(jnp.logical_not(past_used), axis=-1, stable=True).astype(jnp.int32)
    slots = jnp.arange(nt + 1, dtype=jnp.int32)
    order = jnp.concatenate([order, order[..., :1]], axis=-1)
    tiles = jnp.where(slots < n_past[..., None], order, tile_ids[None, :, None])
    grid_spec = pltpu.PrefetchScalarGridSpec(
        num_scalar_prefetch=3,
        grid=(NSA_KV_HEADS, nt),
        in_specs=[pl.BlockSpec((tq, gw), lambda g, i, s, tl, n: (i, g)),
                  pl.BlockSpec((t, HEAD_DIM), lambda g, i, s, tl, n: (0, kcol + g)),
                  pl.BlockSpec((t, HEAD_DIM), lambda g, i, s, tl, n: (0, vcol + g)),
                  pl.BlockSpec((None, tq, LANES), lambda g, i, s, tl, n: (g, i, 0)),
                  pl.BlockSpec((1, HEAD_DIM), lambda g, i, s, tl, n: (0, 0)),
                  pl.BlockSpec((1, HEAD_DIM), lambda g, i, s, tl, n: (0, 0))],
        out_specs=pl.BlockSpec((tq, gw), lambda g, i, s, tl, n: (i, g)),
        scratch_shapes=_flash_scratch(t, tq, tq, NSA_GROUP, HEAD_DIM + LANES, 2, 2))
    return pl.pallas_call(
        functools.partial(_nsa_slc_kernel, tq=tq, t_total=t),
        grid_spec=grid_spec,
        out_shape=jax.ShapeDtypeStruct((t, NSA_DIM), F32),
        compiler_params=_cparams(("arbitrary", "arbitrary")),
        name="nsa_selected",
    )(_nsa_slopes(), tiles.reshape(-1), n_past.reshape(-1), u, u, u, sel,
      q_norm.reshape(1, HEAD_DIM), k_norm.reshape(1, HEAD_DIM))


def nsa_window_merge(u, o_cmp, o_slc, gate_logits, q_norm, k_norm, *, kcol, vcol, tq=256):
    t = u.shape[0]
    tq = min(tq, t)
    assert WINDOW % tq == 0
    gw = NSA_GROUP * HEAD_DIM
    gl = gate_logits.reshape(t, NSA_KV_HEADS, NSA_GROUP * 3)
    gl = jnp.pad(gl, ((0, 0), (0, 0), (0, LANES - NSA_GROUP * 3))).reshape(t, NSA_KV_HEADS * LANES)
    grid_spec = pltpu.PrefetchScalarGridSpec(
        num_scalar_prefetch=1,
        grid=(NSA_KV_HEADS, t // tq),
        in_specs=[pl.BlockSpec((tq, gw), lambda g, i, s: (i, g)),
                  pl.BlockSpec((t, HEAD_DIM), lambda g, i, s: (0, kcol + g)),
                  pl.BlockSpec((t, HEAD_DIM), lambda g, i, s: (0, vcol + g)),
                  pl.BlockSpec((tq, gw), lambda g, i, s: (i, g)),
                  pl.BlockSpec((tq, gw), lambda g, i, s: (i, g)),
                  pl.BlockSpec((tq, LANES), lambda g, i, s: (i, g)),
                  pl.BlockSpec((1, HEAD_DIM), lambda g, i, s: (0, 0)),
                  pl.BlockSpec((1, HEAD_DIM), lambda g, i, s: (0, 0))],
        out_specs=pl.BlockSpec((tq, gw), lambda g, i, s: (i, g)),
        scratch_shapes=_flash_scratch(t, tq, tq, NSA_GROUP, HEAD_DIM, 3, WINDOW // tq + 1))
    return pl.pallas_call(
        functools.partial(_nsa_win_kernel, tq=tq, t_total=t),
        grid_spec=grid_spec,
        out_shape=jax.ShapeDtypeStruct((t, NSA_DIM), MXU_DTYPE),
        compiler_params=_cparams(("arbitrary", "arbitrary")),
        name="nsa_window_merge",
    )(_nsa_slopes(), u, u, u, o_cmp, o_slc, gl, q_norm.reshape(1, HEAD_DIM), k_norm.reshape(1, HEAD_DIM))


def _mx(w):
    return w.astype(MXU_DTYPE)


def conv_moba_layer(x, attn_norm, w_in, conv_w, conv_b, conv_norm_g, conv_norm_b, q_norm, k_norm,
                    w_out, ffn_norm, w_gate, w_up, w_down):
    u = norm_matmul(x, attn_norm, _mx(w_in))
    a = conv_module(u, conv_w, conv_b, conv_norm_g, conv_norm_b)
    o = moba_attention(u, q_norm, k_norm, col0=2 * CONV_CH // HEAD_DIM)
    w_out = _mx(w_out)
    x = proj_residual(x, [(a, w_out[:CONV_CH]), (o, w_out[CONV_CH:])])
    return dense_ffn(x, ffn_norm, _mx(w_gate), _mx(w_up), _mx(w_down))


def nsa_moe_layer(x, attn_norm, w_in, q_norm, kc_norm, ks_norm, kw_norm, cmp_pe_k, cmp_w1_k, cmp_w2_k,
                  cmp_pe_v, cmp_w1_v, cmp_w2_v, w_out, ffn_norm, w_router, w_gate, w_up, w_down):
    t = x.shape[0]
    main = NSA_DIM + 6 * NSA_KV_DIM
    n_gate = 3 * NSA_HEADS
    gate_tile = 512
    w_in_p = jnp.pad(_mx(w_in), ((0, 0), (0, gate_tile - n_gate)))
    u = norm_matmul(x, attn_norm, w_in_p, tn=gate_tile)
    gate_logits = u[:, main:main + n_gate]
    nrows = t // CMP_STRIDE

    def blocks(col):
        v = u[:, col:col + NSA_KV_DIM].reshape(nrows, CMP_STRIDE, NSA_KV_HEADS, HEAD_DIM)
        return v.transpose(2, 0, 1, 3).reshape(NSA_KV_HEADS, nrows, CMP_STRIDE * HEAD_DIM)

    k_cmp = compress(blocks(NSA_DIM), cmp_pe_k, _mx(cmp_w1_k), _mx(cmp_w2_k), kc_norm, normalize=True)
    v_cmp = compress(blocks(NSA_DIM + NSA_KV_DIM), cmp_pe_v, _mx(cmp_w1_v), _mx(cmp_w2_v), kc_norm,
                     normalize=False)
    o_cmp, sel, blocks_used = nsa_compressed(u, k_cmp, v_cmp, q_norm)
    cb = lambda col: col // HEAD_DIM
    o_slc = nsa_selected(u, sel, blocks_used, q_norm, ks_norm, kcol=cb(NSA_DIM + 2 * NSA_KV_DIM),
                         vcol=cb(NSA_DIM + 3 * NSA_KV_DIM))
    o = nsa_window_merge(u, o_cmp, o_slc, gate_logits, q_norm, kw_norm,
                         kcol=cb(NSA_DIM + 4 * NSA_KV_DIM), vcol=cb(NSA_DIM + 5 * NSA_KV_DIM))
    x = proj_residual(x, [(o, _mx(w_out))])
    return moe_ffn(x, ffn_norm, w_router, _mx(w_gate), _mx(w_up), _mx(w_down))


def kernel(x, l0_attn_norm, l0_w_in, l0_conv_w, l0_conv_b, l0_conv_norm_g, l0_conv_norm_b, l0_q_norm, l0_k_norm, l0_w_out, l0_ffn_norm, l0_w_gate, l0_w_up, l0_w_down, l1_attn_norm, l1_w_in, l1_q_norm, l1_kc_norm, l1_ks_norm, l1_kw_norm, l1_cmp_pe_k, l1_cmp_w1_k, l1_cmp_w2_k, l1_cmp_pe_v, l1_cmp_w1_v, l1_cmp_w2_v, l1_w_out, l1_ffn_norm, l1_w_router, l1_w_gate, l1_w_up, l1_w_down, l2_attn_norm, l2_w_in, l2_conv_w, l2_conv_b, l2_conv_norm_g, l2_conv_norm_b, l2_q_norm, l2_k_norm, l2_w_out, l2_ffn_norm, l2_w_gate, l2_w_up, l2_w_down, l3_attn_norm, l3_w_in, l3_q_norm, l3_kc_norm, l3_ks_norm, l3_kw_norm, l3_cmp_pe_k, l3_cmp_w1_k, l3_cmp_w2_k, l3_cmp_pe_v, l3_cmp_w1_v, l3_cmp_w2_v, l3_w_out, l3_ffn_norm, l3_w_router, l3_w_gate, l3_w_up, l3_w_down):
    b, t, d = x.shape
    assert b == 1 and d == D_MODEL
    h = x.reshape(t, d)
    h = conv_moba_layer(h, l0_attn_norm, l0_w_in, l0_conv_w, l0_conv_b, l0_conv_norm_g, l0_conv_norm_b, l0_q_norm, l0_k_norm, l0_w_out, l0_ffn_norm, l0_w_gate, l0_w_up, l0_w_down)
    h = nsa_moe_layer(h, l1_attn_norm, l1_w_in, l1_q_norm, l1_kc_norm, l1_ks_norm, l1_kw_norm, l1_cmp_pe_k, l1_cmp_w1_k, l1_cmp_w2_k, l1_cmp_pe_v, l1_cmp_w1_v, l1_cmp_w2_v, l1_w_out, l1_ffn_norm, l1_w_router, l1_w_gate, l1_w_up, l1_w_down)
    h = conv_moba_layer(h, l2_attn_norm, l2_w_in, l2_conv_w, l2_conv_b, l2_conv_norm_g, l2_conv_norm_b, l2_q_norm, l2_k_norm, l2_w_out, l2_ffn_norm, l2_w_gate, l2_w_up, l2_w_down)
    h = nsa_moe_layer(h, l3_attn_norm, l3_w_in, l3_q_norm, l3_kc_norm, l3_ks_norm, l3_kw_norm, l3_cmp_pe_k, l3_cmp_w1_k, l3_cmp_w2_k, l3_cmp_pe_v, l3_cmp_w1_v, l3_cmp_w2_v, l3_w_out, l3_ffn_norm, l3_w_router, l3_w_gate, l3_w_up, l3_w_down)
    return h.reshape(b, t, d)
```

```python
import functools

import numpy as np
import jax
import jax.numpy as jnp
from jax import lax
from jax.experimental import pallas as pl
from jax.experimental.pallas import tpu as pltpu

D_MODEL = 2048
HEAD_DIM = 128
CONV_CH = D_MODEL // 2
CONV_WIDTH = 31
MOBA_HEADS = (D_MODEL // 2) // HEAD_DIM
MOBA_DIM = MOBA_HEADS * HEAD_DIM
MOBA_BLOCK = 256
MOBA_TOPK = 3
NSA_HEADS = D_MODEL // HEAD_DIM
NSA_KV_HEADS = NSA_HEADS // 4
NSA_GROUP = NSA_HEADS // NSA_KV_HEADS
NSA_DIM = NSA_HEADS * HEAD_DIM
NSA_KV_DIM = NSA_KV_HEADS * HEAD_DIM
CMP_LEN = 32
CMP_STRIDE = 16
CMP_HIDDEN = 256
SLC_BLOCK = 64
SLC_TOPN = 16
WINDOW = 512
D_FF = ((8 * D_MODEL) // 3 + 255) // 256 * 256
N_EXPERTS = 8
TOP_K = 2
EPS = 1e-6
NEG = -1e30

LANES = 128
F32 = jnp.float32
BF16 = jnp.bfloat16
MXU_DTYPE = BF16
HIGHEST = lax.Precision.HIGHEST
VMEM_LIMIT = 56 * 1024 * 1024


def _cparams(sem):
    return pltpu.CompilerParams(dimension_semantics=sem, vmem_limit_bytes=VMEM_LIMIT)


def _rms(x, g):
    return x * lax.rsqrt(jnp.mean(x * x, axis=-1, keepdims=True) + EPS) * g


def _dot(a, b):
    return jnp.dot(a, b, preferred_element_type=F32)


def _dot_nt(a, b, precision=None):
    return lax.dot_general(a, b, (((1,), (1,)), ((), ())), precision=precision,
                           preferred_element_type=F32)


def _sigmoid(x):
    return 1.0 / (1.0 + jnp.exp(-x))


def _norm_matmul_kernel(x_ref, g_ref, w_ref, o_ref, h_ref):
    @pl.when(pl.program_id(1) == 0)
    def _():
        h_ref[...] = _rms(x_ref[...], g_ref[...]).astype(h_ref.dtype)

    o_ref[...] = _dot(h_ref[...], w_ref[...]).astype(o_ref.dtype)


def norm_matmul(x, g, w, *, tm=1024, tn=1024):
    t, d = x.shape
    n = w.shape[1]
    tm = min(tm, t)
    assert t % tm == 0 and n % tn == 0
    return pl.pallas_call(
        _norm_matmul_kernel,
        grid=(t // tm, n // tn),
        in_specs=[pl.BlockSpec((tm, d), lambda i, j: (i, 0)),
                  pl.BlockSpec((1, d), lambda i, j: (0, 0)),
                  pl.BlockSpec((d, tn), lambda i, j: (0, j))],
        out_specs=pl.BlockSpec((tm, tn), lambda i, j: (i, j)),
        out_shape=jax.ShapeDtypeStruct((t, n), F32),
        scratch_shapes=[pltpu.VMEM((tm, d), MXU_DTYPE)],
        compiler_params=_cparams(("arbitrary", "arbitrary")),
        name="norm_matmul",
    )(x, g.reshape(1, d), w)


def _proj_residual_kernel(*refs, n_pairs):
    x_ref = refs[0]
    o_ref = refs[1 + 2 * n_pairs]
    acc = x_ref[...]
    for p in range(n_pairs):
        acc = acc + _dot(refs[1 + 2 * p][...], refs[2 + 2 * p][...])
    o_ref[...] = acc


def proj_residual(x, pairs, *, tm=1024, tn=1024):
    t, d = x.shape
    tm = min(tm, t)
    in_specs = [pl.BlockSpec((tm, tn), lambda i, j: (i, j))]
    args = [x]
    for a, w in pairs:
        k = a.shape[1]
        in_specs.append(pl.BlockSpec((tm, k), lambda i, j: (i, 0)))
        in_specs.append(pl.BlockSpec((k, tn), lambda i, j: (0, j)))
        args += [a, w]
    return pl.pallas_call(
        functools.partial(_proj_residual_kernel, n_pairs=len(pairs)),
        grid=(t // tm, d // tn),
        in_specs=in_specs,
        out_specs=pl.BlockSpec((tm, tn), lambda i, j: (i, j)),
        out_shape=jax.ShapeDtypeStruct((t, d), F32),
        compiler_params=_cparams(("arbitrary", "arbitrary")),
        name="proj_residual",
    )(*args)


CONV_HALO = 32
CONV_ROWS = 32


def _conv_kernel(av_ref, ag_ref, hv_ref, hg_ref, w_ref, b_ref, lg_ref, lb_ref, o_ref, s_ref, *, tq):
    i = pl.program_id(0)
    halo = hv_ref[...] * _sigmoid(hg_ref[...])
    s_ref[0:CONV_HALO, :] = jnp.where(i > 0, halo, 0.0)
    s_ref[CONV_HALO:, :] = av_ref[...] * _sigmoid(ag_ref[...])
    first = CONV_HALO - (CONV_WIDTH - 1)
    for c in range(tq // CONV_ROWS):
        base = c * CONV_ROWS + first
        acc = jnp.broadcast_to(b_ref[...], (CONV_ROWS, CONV_CH))
        for k in range(CONV_WIDTH):
            acc = acc + s_ref[base + k:base + k + CONV_ROWS, :] * w_ref[k:k + 1, :]
        mu = jnp.mean(acc, axis=-1, keepdims=True)
        cen = acc - mu
        var = jnp.mean(cen * cen, axis=-1, keepdims=True)
        y = cen * lax.rsqrt(var + EPS) * lg_ref[...] + lb_ref[...]
        o_ref[c * CONV_ROWS:(c + 1) * CONV_ROWS, :] = (y * _sigmoid(y)).astype(o_ref.dtype)


def conv_module(u, conv_w, conv_b, ln_g, ln_b, *, tq=512):
    t = u.shape[0]
    tq = min(tq, t)
    c = CONV_CH
    hb = tq // CONV_HALO
    w = jnp.pad(conv_w.reshape(CONV_WIDTH, c), ((0, 1), (0, 0)))
    row = lambda v: v.reshape(1, c)
    const = lambda i: (0, 0)
    return pl.pallas_call(
        functools.partial(_conv_kernel, tq=tq),
        grid=(t // tq,),
        in_specs=[pl.BlockSpec((tq, c), lambda i: (i, 0)),
                  pl.BlockSpec((tq, c), lambda i: (i, 1)),
                  pl.BlockSpec((CONV_HALO, c), lambda i: (jnp.maximum(i * hb - 1, 0), 0)),
                  pl.BlockSpec((CONV_HALO, c), lambda i: (jnp.maximum(i * hb - 1, 0), 1)),
                  pl.BlockSpec((CONV_WIDTH + 1, c), const),
                  pl.BlockSpec((1, c), const), pl.BlockSpec((1, c), const), pl.BlockSpec((1, c), const)],
        out_specs=pl.BlockSpec((tq, c), lambda i: (i, 0)),
        out_shape=jax.ShapeDtypeStruct((t, c), MXU_DTYPE),
        scratch_shapes=[pltpu.VMEM((tq + CONV_HALO, c), F32)],
        compiler_params=_cparams(("arbitrary",)),
        name="conv_module",
    )(u, u, u, u, w, row(conv_b), row(ln_g), row(ln_b))


ONES_ROWS = 16
LOG2E = 1.4426950408889634
QK_SCALE_LOG2 = HEAD_DIM ** -0.5 * LOG2E


def _kv_prologue(k_ref, v_ref, kg_ref, kaug_s, vt_s, *, t, tk, blk):
    kn = _rms(k_ref[...], kg_ref[...])
    kaug_s[:, 0:HEAD_DIM] = kn.astype(kaug_s.dtype)
    if blk is not None:
        lane_blk = lax.broadcasted_iota(jnp.int32, (t, LANES), 1)
        key_blk = lax.broadcasted_iota(jnp.int32, (t, LANES), 0) // blk
        kaug_s[:, HEAD_DIM:] = jnp.where(lane_blk == key_blk, 1.0, 0.0).astype(kaug_s.dtype)
    for c in range(t // tk):
        vt_s[c, 0:HEAD_DIM, :] = v_ref[c * tk:(c + 1) * tk, :].T.astype(vt_s.dtype)
        vt_s[c, HEAD_DIM:, :] = jnp.ones((ONES_ROWS, tk), vt_s.dtype)
    return kn


PLAIN, CAUSAL, WINDOW_TAIL = 0, 1, 2


def _tile_pattern(kind, tk, tq):
    key_i = lax.broadcasted_iota(jnp.int32, (tk, tq), 0)
    qry_i = lax.broadcasted_iota(jnp.int32, (tk, tq), 1)
    if kind == CAUSAL:
        return key_i <= qry_i
    if kind == WINDOW_TAIL:
        return key_i > qry_i
    return None


def _fill_bias(sb_s, slopes, kinds, *, tk, tq):
    key_off = lax.broadcasted_iota(jnp.int32, (tk, tq), 0).astype(F32)
    for r, slope in enumerate(slopes):
        bias = (LOG2E * slope) * key_off
        for v, kind in enumerate(kinds):
            pattern = _tile_pattern(kind, tk, tq)
            sb_s[v, r] = bias if pattern is None else jnp.where(pattern, bias, NEG)


def _flash_init(m_s, acc_s):
    m_s[...] = jnp.full_like(m_s, NEG)
    acc_s[...] = jnp.zeros_like(acc_s)


def _qk_tile(j, slot, *, heads, qaug_s, kaug_s, s_s, tq, tk):
    kj = kaug_s[pl.ds(pl.multiple_of(j * tk, tk), tk), :]
    for r in range(heads):
        s_s[slot, r] = _dot_nt(kj, qaug_s[r * tq:(r + 1) * tq, :])


def _softmax_pv_tile(j, slot, variant, *, t0, slopes, vt_s, s_s, sb_s, m_s, acc_s, tq, tk):
    vtj = vt_s[j]
    off = (j * tk - t0).astype(F32)
    for r, slope in enumerate(slopes):
        cols = slice(r * tq, (r + 1) * tq)
        s_r = s_s[slot, r] + sb_s[variant, r]
        c = (LOG2E * slope) * off
        m_old = m_s[:, cols]
        m_new = jnp.maximum(m_old, jnp.max(s_r, axis=0, keepdims=True) + c)
        p_r = jnp.exp2(s_r - (m_new - c)).astype(MXU_DTYPE)
        acc_s[:, cols] = jnp.exp2(m_old - m_new) * acc_s[:, cols] + _dot(vtj, p_r)
        m_s[:, cols] = m_new


SWEEP_UNROLL = 4


def _causal_sweep(qk, spv, tile_at, n_past):
    qk(tile_at(0), 0)

    def visit(k, count):
        for u in range(count):
            qk(tile_at(k + u + 1), (u + 1) % 2)
            spv(tile_at(k + u), u % 2, PLAIN)

    def trip(i, carry):
        visit(SWEEP_UNROLL * i, SWEEP_UNROLL)
        return carry

    lax.fori_loop(0, n_past // SWEEP_UNROLL, trip, 0)
    k = (n_past // SWEEP_UNROLL) * SWEEP_UNROLL
    for rest in range(SWEEP_UNROLL):
        @pl.when(n_past % SWEEP_UNROLL == rest)
        def _():
            visit(k, rest)
            spv(tile_at(k + rest), rest % 2, CAUSAL)


def _flash_result(acc_s, r, tq):
    cols = slice(r * tq, (r + 1) * tq)
    o_t = acc_s[0:HEAD_DIM, cols] / acc_s[HEAD_DIM:HEAD_DIM + 1, cols]
    return o_t.T


def _flash_scratch(t, tq, tk, heads, kaug_cols, n_patterns, n_slots):
    nq = heads * tq
    return [pltpu.VMEM((t, kaug_cols), MXU_DTYPE),
            pltpu.VMEM((t // tk, HEAD_DIM + ONES_ROWS, tk), MXU_DTYPE),
            pltpu.VMEM((nq, kaug_cols), MXU_DTYPE),
            pltpu.VMEM((n_patterns, heads, tk, tq), F32),
            pltpu.VMEM((n_slots, heads, tk, tq), F32),
            pltpu.VMEM((1, nq), F32),
            pltpu.VMEM((HEAD_DIM + ONES_ROWS, nq), F32)]


def _flash_stages(*, t0, slopes, qaug_s, kaug_s, vt_s, sb_s, s_s, m_s, acc_s, tq, tk):
    qk = functools.partial(_qk_tile, heads=len(slopes), qaug_s=qaug_s, kaug_s=kaug_s, s_s=s_s, tq=tq, tk=tk)
    spv = functools.partial(_softmax_pv_tile, t0=t0, slopes=slopes, vt_s=vt_s, s_s=s_s, sb_s=sb_s, m_s=m_s,
                            acc_s=acc_s, tq=tq, tk=tk)
    return qk, spv


def _moba_kernel(slopes_ref, q_ref, k_ref, v_ref, qg_ref, kg_ref, o_ref,
                 kaug_s, vt_s, qaug_s, sb_s, s_s, m_s, acc_s, km_s, *, tq, t_total):
    h = pl.program_id(0)
    qi = pl.program_id(1)
    nb = t_total // MOBA_BLOCK
    tk = tq

    slopes = [slopes_ref[h]]

    @pl.when(qi == 0)
    def _():
        kn = _kv_prologue(k_ref, v_ref, kg_ref, kaug_s, vt_s, t=t_total, tk=tk, blk=MOBA_BLOCK)
        km_s[...] = jnp.zeros_like(km_s)
        km_s[0:nb, :] = jnp.mean(kn.reshape(nb, MOBA_BLOCK, HEAD_DIM), axis=1)
        _fill_bias(sb_s, slopes, (PLAIN, CAUSAL), tk=tk, tq=tq)

    qn = _rms(q_ref[...], qg_ref[...])
    nb8 = -(-nb // 8) * 8
    gate = _dot_nt(km_s[0:nb8, :], qn, precision=HIGHEST)
    blk = lax.broadcasted_iota(jnp.int32, (nb8, tq), 0)
    blk_f = blk.astype(F32)
    qb = (qi * tq + lax.broadcasted_iota(jnp.int32, (nb8, tq), 1)) // MOBA_BLOCK
    past = blk < qb
    gate = jnp.where(past, gate, -jnp.inf)
    sel = blk == qb
    for _ in range(MOBA_TOPK):
        top = jnp.max(gate, axis=0, keepdims=True)
        first = jnp.min(jnp.where(gate == top, blk_f, float(nb8)), axis=0, keepdims=True)
        pick = blk_f == first
        sel = sel | (pick & past)
        gate = jnp.where(pick, -jnp.inf, gate)
    sel_bias = jnp.where(sel, 0.0, NEG)
    if nb8 < LANES:
        sel_bias = jnp.concatenate([sel_bias, jnp.full((LANES - nb8, tq), NEG, F32)], axis=0)
    qaug_s[:, 0:HEAD_DIM] = (qn * QK_SCALE_LOG2).astype(qaug_s.dtype)
    qaug_s[:, HEAD_DIM:] = sel_bias.T.astype(qaug_s.dtype)

    _flash_init(m_s, acc_s)
    qk, spv = _flash_stages(t0=qi * tq, slopes=slopes, qaug_s=qaug_s, kaug_s=kaug_s, vt_s=vt_s, sb_s=sb_s,
                            s_s=s_s, m_s=m_s, acc_s=acc_s, tq=tq, tk=tk)
    _causal_sweep(qk, spv, lambda k: k, qi)
    o_ref[...] = _flash_result(acc_s, 0, tq).astype(o_ref.dtype)


def moba_attention(u, q_norm, k_norm, *, col0, tq=512):
    t = u.shape[0]
    tq = min(tq, t)
    assert t // MOBA_BLOCK <= LANES and tq % MOBA_BLOCK == 0
    nh = MOBA_HEADS
    slopes = jnp.asarray(2.0 ** (-8.0 * np.arange(1, nh + 1) / nh), dtype=F32)
    grid_spec = pltpu.PrefetchScalarGridSpec(
        num_scalar_prefetch=1,
        grid=(nh, t // tq),
        in_specs=[pl.BlockSpec((tq, HEAD_DIM), lambda h, i, s: (i, col0 + h)),
                  pl.BlockSpec((t, HEAD_DIM), lambda h, i, s: (0, col0 + nh + h)),
                  pl.BlockSpec((t, HEAD_DIM), lambda h, i, s: (0, col0 + 2 * nh + h)),
                  pl.BlockSpec((1, HEAD_DIM), lambda h, i, s: (0, 0)),
                  pl.BlockSpec((1, HEAD_DIM), lambda h, i, s: (0, 0))],
        out_specs=pl.BlockSpec((tq, HEAD_DIM), lambda h, i, s: (i, h)),
        scratch_shapes=_flash_scratch(t, tq, tq, 1, HEAD_DIM + LANES, 2, 2) + [pltpu.VMEM((LANES, HEAD_DIM), F32)])
    return pl.pallas_call(
        functools.partial(_moba_kernel, tq=tq, t_total=t),
        grid_spec=grid_spec,
        out_shape=jax.ShapeDtypeStruct((t, MOBA_DIM), MXU_DTYPE),
        compiler_params=_cparams(("arbitrary", "arbitrary")),
        name="moba_attention",
    )(slopes, u, u, u, q_norm.reshape(1, HEAD_DIM), k_norm.reshape(1, HEAD_DIM))


def _swiglu_step(h, wg_ref, wu_ref, wd_ref):
    a = _dot(h, wg_ref[...])
    b = _dot(h, wu_ref[...])
    mid = (a * _sigmoid(a) * b).astype(MXU_DTYPE)
    return _dot(mid, wd_ref[...])


def _dense_ffn_kernel(x_ref, g_ref, wg_ref, wu_ref, wd_ref, o_ref, h_ref):
    @pl.when(pl.program_id(1) == 0)
    def _():
        x = x_ref[...]
        h_ref[...] = _rms(x, g_ref[...]).astype(h_ref.dtype)
        o_ref[...] = x

    o_ref[...] += _swiglu_step(h_ref[...], wg_ref, wu_ref, wd_ref)


def dense_ffn(x, g, wg, wu, wd, *, tm=512, tf=512):
    t, d = x.shape
    f = wg.shape[1]
    tm = min(tm, t)
    return pl.pallas_call(
        _dense_ffn_kernel,
        grid=(t // tm, f // tf),
        in_specs=[pl.BlockSpec((tm, d), lambda i, j: (i, 0)),
                  pl.BlockSpec((1, d), lambda i, j: (0, 0)),
                  pl.BlockSpec((d, tf), lambda i, j: (0, j)),
                  pl.BlockSpec((d, tf), lambda i, j: (0, j)),
                  pl.BlockSpec((tf, d), lambda i, j: (j, 0))],
        out_specs=pl.BlockSpec((tm, d), lambda i, j: (i, 0)),
        out_shape=jax.ShapeDtypeStruct((t, d), F32),
        scratch_shapes=[pltpu.VMEM((tm, d), MXU_DTYPE)],
        compiler_params=_cparams(("arbitrary", "arbitrary")),
        name="dense_ffn",
    )(x, g.reshape(1, d), wg, wu, wd)


def _grouped_ffn_kernel(te_ref, rows_ref, x_ref, wg_ref, wu_ref, wd_ref, o_ref, h_ref, *, tm):
    rows = rows_ref[pl.program_id(0)]
    half = tm // 2

    @pl.when(pl.program_id(1) == 0)
    def _():
        h_ref[...] = x_ref[...].astype(h_ref.dtype)
        o_ref[...] = jnp.zeros_like(o_ref)

    @pl.when(rows > half)
    def _():
        o_ref[...] += _swiglu_step(h_ref[...], wg_ref, wu_ref, wd_ref)

    @pl.when((rows > 0) & (rows <= half))
    def _():
        o_ref[0:half, :] += _swiglu_step(h_ref[0:half, :], wg_ref, wu_ref, wd_ref)


def grouped_ffn(xs, tile_expert, tile_rows, wg, wu, wd, *, tm, tf=512):
    m, d = xs.shape
    f = wg.shape[2]
    nf = f // tf

    def fcol(i, j, rows):
        return jnp.where(rows[i] > 0, j, nf - 1)

    grid_spec = pltpu.PrefetchScalarGridSpec(
        num_scalar_prefetch=2,
        grid=(m // tm, nf),
        in_specs=[pl.BlockSpec((tm, d), lambda i, j, te, rows: (i, 0)),
                  pl.BlockSpec((None, d, tf), lambda i, j, te, rows: (te[i], 0, fcol(i, j, rows))),
                  pl.BlockSpec((None, d, tf), lambda i, j, te, rows: (te[i], 0, fcol(i, j, rows))),
                  pl.BlockSpec((None, tf, d), lambda i, j, te, rows: (te[i], fcol(i, j, rows), 0))],
        out_specs=pl.BlockSpec((tm, d), lambda i, j, te, rows: (i, 0)),
        scratch_shapes=[pltpu.VMEM((tm, d), MXU_DTYPE)])
    return pl.pallas_call(
        functools.partial(_grouped_ffn_kernel, tm=tm),
        grid_spec=grid_spec,
        out_shape=jax.ShapeDtypeStruct((m, d), F32),
        compiler_params=_cparams(("arbitrary", "arbitrary")),
        name="grouped_ffn",
    )(tile_expert, tile_rows, xs, wg, wu, wd)


META_E, META_W, META_RANK = 0, 2, 4


def _router_kernel(x_ref, g_ref, wr_ref, h_ref, meta_ref, cnt_ref, carry_s, *, tm):
    @pl.when(pl.program_id(0) == 0)
    def _():
        carry_s[...] = jnp.zeros_like(carry_s)

    h = _rms(x_ref[...], g_ref[...])
    h_ref[...] = h
    logits = jnp.dot(h, wr_ref[...], precision=HIGHEST, preferred_element_type=F32)
    lane = lax.broadcasted_iota(jnp.int32, (tm, LANES), 1).astype(F32)
    lg = jnp.where(lane < N_EXPERTS, logits, -jnp.inf)

    def take_top(v):
        top = jnp.max(v, axis=-1, keepdims=True)
        idx = jnp.min(jnp.where(v == top, lane, float(LANES)), axis=-1, keepdims=True)
        return top, idx

    m1, i1 = take_top(lg)
    m2, i2 = take_top(jnp.where(lane == i1, -jnp.inf, lg))
    e = jnp.exp(m2 - m1)
    w1 = 1.0 / (1.0 + e)
    w2 = e / (1.0 + e)
    oh1 = lane == i1
    oh2 = lane == i2
    oh = jnp.where(oh1 | oh2, 1.0, 0.0)
    r = lax.broadcasted_iota(jnp.int32, (tm, tm), 0)
    c = lax.broadcasted_iota(jnp.int32, (tm, tm), 1)
    lower = jnp.where(c < r, 1.0, 0.0).astype(BF16)
    before = _dot(lower, oh.astype(BF16)) + carry_s[...]
    rank1 = jnp.sum(jnp.where(oh1, before, 0.0), axis=-1, keepdims=True)
    rank2 = jnp.sum(jnp.where(oh2, before, 0.0), axis=-1, keepdims=True)
    carry_s[...] += jnp.sum(oh, axis=0, keepdims=True)
    meta = jnp.zeros((tm, LANES), F32)
    for k, v in ((META_E, i1), (META_E + 1, i2), (META_W, w1), (META_W + 1, w2),
                 (META_RANK, rank1), (META_RANK + 1, rank2)):
        meta = jnp.where(lane == k, v, meta)
    meta_ref[...] = meta
    cnt_ref[...] = jnp.broadcast_to(carry_s[...], cnt_ref.shape)


def moe_router(x, g, w_router, *, tm=512):
    t, d = x.shape
    tm = min(tm, t)
    wr = jnp.pad(w_router, ((0, 0), (0, LANES - N_EXPERTS)))
    return pl.pallas_call(
        functools.partial(_router_kernel, tm=tm),
        grid=(t // tm,),
        in_specs=[pl.BlockSpec((tm, d), lambda i: (i, 0)),
                  pl.BlockSpec((1, d), lambda i: (0, 0)),
                  pl.BlockSpec((d, LANES), lambda i: (0, 0))],
        out_specs=[pl.BlockSpec((tm, d), lambda i: (i, 0)),
                   pl.BlockSpec((tm, LANES), lambda i: (i, 0)),
                   pl.BlockSpec((8, LANES), lambda i: (0, 0))],
        out_shape=[jax.ShapeDtypeStruct((t, d), F32),
                   jax.ShapeDtypeStruct((t, LANES), F32),
                   jax.ShapeDtypeStruct((8, LANES), F32)],
        scratch_shapes=[pltpu.VMEM((1, LANES), F32)],
        compiler_params=_cparams(("arbitrary",)),
        name="moe_router",
    )(x, g.reshape(1, d), wr)


DMA_LOOP_UNROLL = 8


def _dispatch_kernel(p1_ref, p2_ref, h_ref, init_ref, xs_ref, sem, *, tr):
    del init_ref
    base = pl.program_id(0) * tr

    def copies(r, tok):
        src = h_ref.at[pl.ds(r, 1)]
        return (pltpu.make_async_copy(src, xs_ref.at[pl.ds(p1_ref[tok], 1)], sem.at[0]),
                pltpu.make_async_copy(src, xs_ref.at[pl.ds(p2_ref[tok], 1)], sem.at[1]))

    def issue(r, carry):
        for c in copies(r, base + r):
            c.start()
        return carry

    lax.fori_loop(0, tr, issue, 0, unroll=DMA_LOOP_UNROLL)

    def drain(r, carry):
        for c in copies(0, 0):
            c.wait()
        return carry

    lax.fori_loop(0, tr, drain, 0, unroll=DMA_LOOP_UNROLL)


def moe_dispatch(h, pos1, pos2, m_pad, *, tr=256):
    t, d = h.shape
    tr = min(tr, t)
    grid_spec = pltpu.PrefetchScalarGridSpec(
        num_scalar_prefetch=2,
        grid=(t // tr,),
        in_specs=[pl.BlockSpec((tr, d), lambda i, p1, p2: (i, 0)),
                  pl.BlockSpec(memory_space=pl.ANY)],
        out_specs=pl.BlockSpec(memory_space=pl.ANY),
        scratch_shapes=[pltpu.SemaphoreType.DMA((2,))])
    return pl.pallas_call(
        functools.partial(_dispatch_kernel, tr=tr),
        grid_spec=grid_spec,
        out_shape=jax.ShapeDtypeStruct((m_pad, d), h.dtype),
        input_output_aliases={3: 0},
        compiler_params=_cparams(("arbitrary",)),
        name="moe_dispatch",
    )(pos1, pos2, h, jnp.zeros((m_pad, d), h.dtype))


def _combine_kernel(p1_ref, p2_ref, x_ref, meta_ref, y_ref, o_ref, y1_s, y2_s, sem, *, tr):
    base = pl.program_id(0) * tr

    def copies(r, tok):
        return (pltpu.make_async_copy(y_ref.at[pl.ds(p1_ref[tok], 1)], y1_s.at[pl.ds(r, 1)], sem.at[0]),
                pltpu.make_async_copy(y_ref.at[pl.ds(p2_ref[tok], 1)], y2_s.at[pl.ds(r, 1)], sem.at[1]))

    def issue(r, carry):
        for c in copies(r, base + r):
            c.start()
        return carry

    lax.fori_loop(0, tr, issue, 0, unroll=DMA_LOOP_UNROLL)

    def drain(r, carry):
        for c in copies(0, 0):
            c.wait()
        return carry

    lax.fori_loop(0, tr, drain, 0, unroll=DMA_LOOP_UNROLL)
    meta = meta_ref[...]
    w1 = meta[:, META_W:META_W + 1]
    w2 = meta[:, META_W + 1:META_W + 2]
    o_ref[...] = x_ref[...] + w1 * y1_s[...] + w2 * y2_s[...]


def moe_combine(x, meta, y, pos1, pos2, *, tr=256):
    t, d = x.shape
    tr = min(tr, t)
    grid_spec = pltpu.PrefetchScalarGridSpec(
        num_scalar_prefetch=2,
        grid=(t // tr,),
        in_specs=[pl.BlockSpec((tr, d), lambda i, p1, p2: (i, 0)),
                  pl.BlockSpec((tr, LANES), lambda i, p1, p2: (i, 0)),
                  pl.BlockSpec(memory_space=pl.ANY)],
        out_specs=pl.BlockSpec((tr, d), lambda i, p1, p2: (i, 0)),
        scratch_shapes=[pltpu.VMEM((tr, d), F32), pltpu.VMEM((tr, d), F32),
                        pltpu.SemaphoreType.DMA((2,))])
    return pl.pallas_call(
        functools.partial(_combine_kernel, tr=tr),
        grid_spec=grid_spec,
        out_shape=jax.ShapeDtypeStruct((t, d), F32),
        compiler_params=_cparams(("arbitrary",)),
        name="moe_combine",
    )(pos1, pos2, x, meta, y)


def moe_ffn(x, ffn_norm, w_router, wg, wu, wd, *, tm=512):
    t, d = x.shape
    tm = min(tm, t)
    h, meta, cnt = moe_router(x, ffn_norm, w_router)
    counts = cnt[0, :N_EXPERTS].astype(jnp.int32)
    padded = (counts + tm - 1) // tm * tm
    ends = jnp.cumsum(padded)
    starts = ends - padded
    ids = meta[:, META_E:META_E + 2].astype(jnp.int32)
    ranks = meta[:, META_RANK:META_RANK + 2].astype(jnp.int32)
    pos = starts[ids] + ranks
    pos1, pos2 = pos[:, 0], pos[:, 1]
    m_pad = TOP_K * t + N_EXPERTS * tm
    n_tiles = m_pad // tm
    tile_start = jnp.arange(n_tiles, dtype=jnp.int32) * tm
    tile_expert = jnp.sum((tile_start[:, None] >= ends[None, :]).astype(jnp.int32), axis=1)
    last_expert = jnp.sum((ends[-1] - 1 >= ends).astype(jnp.int32))
    used = tile_start < ends[-1]
    tile_expert = jnp.where(used, tile_expert, last_expert).astype(jnp.int32)
    tile_rows = jnp.clip((starts + counts)[tile_expert] - tile_start, 0, tm)
    tile_rows = jnp.where(used, tile_rows, 0).astype(jnp.int32)
    xs = moe_dispatch(h, pos1, pos2, m_pad)
    y = grouped_ffn(xs, tile_expert, tile_rows, wg, wu, wd, tm=tm)
    return moe_combine(x, meta, y, pos1, pos2)


def _compress_kernel(c_ref, pe_ref, w1_ref, w2_ref, ng_ref, o_ref, *, nrows, half, normalize):
    c = c_ref[...].astype(MXU_DTYPE)
    top = _dot(c, w1_ref[0:half, :])
    bot = _dot(c, w1_ref[half:, :])
    bot_next = pltpu.roll(bot, nrows - 1, 0)
    pe_term = _dot(pe_ref[...].astype(MXU_DTYPE), w1_ref[...])
    hid = top + bot_next + pe_term
    out = _dot((hid * _sigmoid(hid)).astype(MXU_DTYPE), w2_ref[...])
    if normalize:
        out = _rms(out, ng_ref[...])
    valid = lax.broadcasted_iota(jnp.int32, (nrows, 1), 0) < nrows - 1
    o_ref[...] = jnp.where(valid, out, 0.0)


def compress(c, pe, w1, w2, norm_g, *, normalize):
    g, nrows, half = c.shape
    return pl.pallas_call(
        functools.partial(_compress_kernel, nrows=nrows, half=half, normalize=normalize),
        grid=(g,),
        in_specs=[pl.BlockSpec((None, nrows, half), lambda i: (i, 0, 0)),
                  pl.BlockSpec((1, 2 * half), lambda i: (0, 0)),
                  pl.BlockSpec((2 * half, CMP_HIDDEN), lambda i: (0, 0)),
                  pl.BlockSpec((CMP_HIDDEN, HEAD_DIM), lambda i: (0, 0)),
                  pl.BlockSpec((1, HEAD_DIM), lambda i: (0, 0))],
        out_specs=pl.BlockSpec((None, nrows, HEAD_DIM), lambda i: (i, 0, 0)),
        out_shape=jax.ShapeDtypeStruct((g, nrows, HEAD_DIM), F32),
        compiler_params=_cparams(("arbitrary",)),
        name="nsa_compress",
    )(c, pe.reshape(1, 2 * half), w1, w2, norm_g.reshape(1, HEAD_DIM))


def _nsa_cmp_kernel(slopes_ref, q_ref, kc_ref, vc_ref, qg_ref, ov_ref, oc_ref, sel_ref, used_ref, kc_s, vct_s,
                    psum_s, *, tq, ncmp, nsel):
    g = pl.program_id(0)
    qi = pl.program_id(1)

    @pl.when(qi == 0)
    def _():
        kc_s[...] = kc_ref[...].astype(kc_s.dtype)
        vct_s[...] = vc_ref[...].T.astype(vct_s.dtype)

    def heads(nk):
        tpos = qi * tq + lax.broadcasted_iota(jnp.int32, (nk, tq), 1)
        cend = lax.broadcasted_iota(jnp.int32, (nk, tq), 0) * CMP_STRIDE + (CMP_LEN - 1)
        dist = tpos - cend
        ok = dist >= 0
        dist_f = dist.astype(F32)
        any_ok = jnp.where(tpos[0:1, :] >= CMP_LEN - 1, 1.0, 0.0)
        psum = jnp.zeros((nk, tq), F32)
        for r in range(NSA_GROUP):
            cols = slice(r * HEAD_DIM, (r + 1) * HEAD_DIM)
            qn = _rms(q_ref[:, cols], qg_ref[...])
            s = _dot_nt(kc_s[0:nk, :], (qn * HEAD_DIM ** -0.5).astype(MXU_DTYPE))
            s = jnp.where(ok, s - slopes_ref[g * NSA_GROUP + r] * dist_f, NEG)
            p = jnp.exp(s - jnp.max(s, axis=0, keepdims=True))
            p = p * (any_ok / jnp.sum(p, axis=0, keepdims=True))
            oc_ref[:, cols] = _dot(vct_s[:, 0:nk], p.astype(MXU_DTYPE)).T.astype(oc_ref.dtype)
            psum = psum + p
        psum_s[0:nk, :] = psum
        if nk < ncmp:
            psum_s[nk:, :] = jnp.zeros((ncmp - nk, tq), F32)

    n_prefix = ncmp // LANES if ncmp % LANES == 0 else 1
    step_keys = ncmp // n_prefix
    n_visible = (qi * tq + tq - CMP_LEN) // CMP_STRIDE + 1
    need = jnp.minimum((n_visible + step_keys - 1) // step_keys, n_prefix)
    for v in range(1, n_prefix + 1):
        @pl.when(need == v)
        def _():
            heads(v * step_keys)

    psum = psum_s[...]
    p_hi = psum.astype(MXU_DTYPE)
    p_lo = (psum - p_hi.astype(F32)).astype(MXU_DTYPE)
    imp = _dot(ov_ref[...], p_hi) + _dot(ov_ref[...], p_lo)
    blk = lax.broadcasted_iota(jnp.int32, (LANES, tq), 0)
    qb = (qi * tq + lax.broadcasted_iota(jnp.int32, (LANES, tq), 1)) // SLC_BLOCK
    forced = (blk == 0) | (blk == qb) | (blk == qb - 1)
    visible = blk <= qb
    imp = jnp.where(visible & jnp.logical_not(forced), imp, -jnp.inf)
    blk_f = blk.astype(F32)
    sel = forced
    for _ in range(min(SLC_TOPN, nsel) - 3):
        top = jnp.max(imp, axis=0, keepdims=True)
        first = jnp.min(jnp.where(imp == top, blk_f, float(LANES)), axis=0, keepdims=True)
        pick = blk_f == first
        sel = sel | pick
        imp = jnp.where(pick, -jnp.inf, imp)
    sel_q = jnp.where(sel & visible, 1.0, 0.0).T
    sel_ref[...] = sel_q.astype(sel_ref.dtype)
    used_ref[...] = jnp.broadcast_to(jnp.max(sel_q, axis=0, keepdims=True), used_ref.shape)


def nsa_compressed(u, k_cmp, v_cmp, q_norm, *, tq=256):
    t = u.shape[0]
    tq = min(tq, t)
    ncmp = k_cmp.shape[1]
    nsel = t // SLC_BLOCK
    assert nsel <= LANES
    nstart = np.arange(ncmp)[:, None] * CMP_STRIDE
    lo = np.arange(LANES)[None, :] * SLC_BLOCK
    overlap = ((nstart < lo + SLC_BLOCK) & (nstart + CMP_LEN > lo) & (np.arange(ncmp)[:, None] < ncmp - 1))
    ov_t = jnp.asarray(overlap.T.astype(np.float32)).astype(MXU_DTYPE)
    gw = NSA_GROUP * HEAD_DIM
    grid_spec = pltpu.PrefetchScalarGridSpec(
        num_scalar_prefetch=1,
        grid=(NSA_KV_HEADS, t // tq),
        in_specs=[pl.BlockSpec((tq, gw), lambda g, i, s: (i, g)),
                  pl.BlockSpec((None, ncmp, HEAD_DIM), lambda g, i, s: (g, 0, 0)),
                  pl.BlockSpec((None, ncmp, HEAD_DIM), lambda g, i, s: (g, 0, 0)),
                  pl.BlockSpec((1, HEAD_DIM), lambda g, i, s: (0, 0)),
                  pl.BlockSpec((LANES, ncmp), lambda g, i, s: (0, 0))],
        out_specs=[pl.BlockSpec((tq, gw), lambda g, i, s: (i, g)),
                   pl.BlockSpec((None, tq, LANES), lambda g, i, s: (g, i, 0)),
                   pl.BlockSpec((None, None, 8, LANES), lambda g, i, s: (g, i, 0, 0))],
        scratch_shapes=[pltpu.VMEM((ncmp, HEAD_DIM), MXU_DTYPE), pltpu.VMEM((HEAD_DIM, ncmp), MXU_DTYPE),
                        pltpu.VMEM((ncmp, tq), F32)])
    return pl.pallas_call(
        functools.partial(_nsa_cmp_kernel, tq=tq, ncmp=ncmp, nsel=nsel),
        grid_spec=grid_spec,
        out_shape=[jax.ShapeDtypeStruct((t, NSA_DIM), F32),
                   jax.ShapeDtypeStruct((NSA_KV_HEADS, t, LANES), MXU_DTYPE),
                   jax.ShapeDtypeStruct((NSA_KV_HEADS, t // tq, 8, LANES), F32)],
        compiler_params=_cparams(("arbitrary", "arbitrary")),
        name="nsa_compressed",
    )(_nsa_slopes(), u, k_cmp, v_cmp, q_norm.reshape(1, HEAD_DIM), ov_t)


def _nsa_slopes():
    return jnp.asarray(2.0 ** (-8.0 * np.arange(1, NSA_HEADS + 1) / NSA_HEADS), dtype=F32)


def _group_queries(q_ref, qg_ref, qaug_s, tq):
    for r in range(NSA_GROUP):
        qn = _rms(q_ref[:, r * HEAD_DIM:(r + 1) * HEAD_DIM], qg_ref[...])
        qaug_s[r * tq:(r + 1) * tq, 0:HEAD_DIM] = (qn * QK_SCALE_LOG2).astype(qaug_s.dtype)


def _nsa_slc_kernel(slopes_ref, tiles_ref, npast_ref, q_ref, k_ref, v_ref, sel_ref, qg_ref, kg_ref, o_ref,
                    kaug_s, vt_s, qaug_s, sb_s, s_s, m_s, acc_s, *, tq, t_total):
    g = pl.program_id(0)
    qi = pl.program_id(1)
    tk = tq
    row_len = t_total // tk + 1
    slopes = [slopes_ref[g * NSA_GROUP + r] for r in range(NSA_GROUP)]

    @pl.when(qi == 0)
    def _():
        _kv_prologue(k_ref, v_ref, kg_ref, kaug_s, vt_s, t=t_total, tk=tk, blk=SLC_BLOCK)
        _fill_bias(sb_s, slopes, (PLAIN, CAUSAL), tk=tk, tq=tq)

    _group_queries(q_ref, qg_ref, qaug_s, tq)
    sel_bias = jnp.where(sel_ref[...].astype(F32) > 0.5, 0.0, NEG).astype(qaug_s.dtype)
    for r in range(NSA_GROUP):
        qaug_s[r * tq:(r + 1) * tq, HEAD_DIM:] = sel_bias
    _flash_init(m_s, acc_s)
    qk, spv = _flash_stages(t0=qi * tq, slopes=slopes, qaug_s=qaug_s, kaug_s=kaug_s, vt_s=vt_s, sb_s=sb_s,
                            s_s=s_s, m_s=m_s, acc_s=acc_s, tq=tq, tk=tk)
    row = g * pl.num_programs(1) + qi
    _causal_sweep(qk, spv, lambda k: tiles_ref[row * row_len + k], npast_ref[row])
    for r in range(NSA_GROUP):
        o_ref[:, r * HEAD_DIM:(r + 1) * HEAD_DIM] = _flash_result(acc_s, r, tq).astype(o_ref.dtype)


def _nsa_win_kernel(slopes_ref, q_ref, k_ref, v_ref, oc_ref, os_ref, gl_ref, qg_ref, kg_ref, o_ref,
                    kaug_s, vt_s, qaug_s, sb_s, s_s, m_s, acc_s, *, tq, t_total):
    g = pl.program_id(0)
    qi = pl.program_id(1)
    tk = tq
    nw = WINDOW // tk
    slopes = [slopes_ref[g * NSA_GROUP + r] for r in range(NSA_GROUP)]

    @pl.when(qi == 0)
    def _():
        _kv_prologue(k_ref, v_ref, kg_ref, kaug_s, vt_s, t=t_total, tk=tk, blk=None)
        _fill_bias(sb_s, slopes, (PLAIN, CAUSAL, WINDOW_TAIL), tk=tk, tq=tq)

    _group_queries(q_ref, qg_ref, qaug_s, tq)
    _flash_init(m_s, acc_s)
    qk, spv = _flash_stages(t0=qi * tq, slopes=slopes, qaug_s=qaug_s, kaug_s=kaug_s, vt_s=vt_s, sb_s=sb_s,
                            s_s=s_s, m_s=m_s, acc_s=acc_s, tq=tq, tk=tk)
    kinds = [WINDOW_TAIL] + [PLAIN] * (nw - 1) + [CAUSAL]

    @pl.when(qi >= nw)
    def _():
        for i in range(nw + 1):
            qk(qi - nw + i, i)
        for i, kind in enumerate(kinds):
            spv(qi - nw + i, i, kind)

    @pl.when(qi < nw)
    def _():
        for back in range(nw - 1, -1, -1):
            @pl.when(qi >= back)
            def _():
                qk(qi - back, 0)
                spv(qi - back, 0, kinds[nw - back])
    gates = _sigmoid(gl_ref[...])
    for r in range(NSA_GROUP):
        cols = slice(r * HEAD_DIM, (r + 1) * HEAD_DIM)
        lane0 = 3 * r
        o = (gates[:, lane0:lane0 + 1] * oc_ref[:, cols] + gates[:, lane0 + 1:lane0 + 2] * os_ref[:, cols]
             + gates[:, lane0 + 2:lane0 + 3] * _flash_result(acc_s, r, tq))
        o_ref[:, cols] = o.astype(o_ref.dtype)


def nsa_selected(u, sel, blocks_used, q_norm, k_norm, *, kcol, vcol, tq=256):
    t = u.shape[0]
    tq = min(tq, t)
    gw = NSA_GROUP * HEAD_DIM
    per_tile = tq // SLC_BLOCK
    nt = t // tq
    used = blocks_used[:, :, 0, :t // SLC_BLOCK].reshape(NSA_KV_HEADS, nt, nt, per_tile)
    tile_ids = jnp.arange(nt, dtype=jnp.int32)
    past_used = (jnp.max(used, axis=-1) > 0) & (tile_ids[None, None, :] < tile_ids[None, :, None])
    n_past = jnp.sum(past_used, axis=-1).astype(jnp.int32)
    order = jnp.argsort(jnp.logical_not(past_used), axis=-1, stable=True).astype(jnp.int32)
    slots = jnp.arange(nt + 1, dtype=jnp.int32)
    order = jnp.concatenate([order, order[..., :1]], axis=-1)
    tiles = jnp.where(slots < n_past[..., None], order, tile_ids[None, :, None])
    grid_spec = pltpu.PrefetchScalarGridSpec(
        num_scalar_prefetch=3,
        grid=(NSA_KV_HEADS, nt),
        in_specs=[pl.BlockSpec((tq, gw), lambda g, i, s, tl, n: (i, g)),
                  pl.BlockSpec((t, HEAD_DIM), lambda g, i, s, tl, n: (0, kcol + g)),
                  pl.BlockSpec((t, HEAD_DIM), lambda g, i, s, tl, n: (0, vcol + g)),
                  pl.BlockSpec((None, tq, LANES), lambda g, i, s, tl, n: (g, i, 0)),
                  pl.BlockSpec((1, HEAD_DIM), lambda g, i, s, tl, n: (0, 0)),
                  pl.BlockSpec((1, HEAD_DIM), lambda g, i, s, tl, n: (0, 0))],
        out_specs=pl.BlockSpec((tq, gw), lambda g, i, s, tl, n: (i, g)),
        scratch_shapes=_flash_scratch(t, tq, tq, NSA_GROUP, HEAD_DIM + LANES, 2, 2))
    return pl.pallas_call(
        functools.partial(_nsa_slc_kernel, tq=tq, t_total=t),
        grid_spec=grid_spec,
        out_shape=jax.ShapeDtypeStruct((t, NSA_DIM), F32),
        compiler_params=_cparams(("arbitrary", "arbitrary")),
        name="nsa_selected",
    )(_nsa_slopes(), tiles.reshape(-1), n_past.reshape(-1), u, u, u, sel,
      q_norm.reshape(1, HEAD_DIM), k_norm.reshape(1, HEAD_DIM))


def nsa_window_merge(u, o_cmp, o_slc, gate_logits, q_norm, k_norm, *, kcol, vcol, tq=256):
    t = u.shape[0]
    tq = min(tq, t)
    assert WINDOW % tq == 0
    gw = NSA_GROUP * HEAD_DIM
    gl = gate_logits.reshape(t, NSA_KV_HEADS, NSA_GROUP * 3)
    gl = jnp.pad(gl, ((0, 0), (0, 0), (0, LANES - NSA_GROUP * 3))).reshape(t, NSA_KV_HEADS * LANES)
    grid_spec = pltpu.PrefetchScalarGridSpec(
        num_scalar_prefetch=1,
        grid=(NSA_KV_HEADS, t // tq),
        in_specs=[pl.BlockSpec((tq, gw), lambda g, i, s: (i, g)),
                  pl.BlockSpec((t, HEAD_DIM), lambda g, i, s: (0, kcol + g)),
                  pl.BlockSpec((t, HEAD_DIM), lambda g, i, s: (0, vcol + g)),
                  pl.BlockSpec((tq, gw), lambda g, i, s: (i, g)),
                  pl.BlockSpec((tq, gw), lambda g, i, s: (i, g)),
                  pl.BlockSpec((tq, LANES), lambda g, i, s: (i, g)),
                  pl.BlockSpec((1, HEAD_DIM), lambda g, i, s: (0, 0)),
                  pl.BlockSpec((1, HEAD_DIM), lambda g, i, s: (0, 0))],
        out_specs=pl.BlockSpec((tq, gw), lambda g, i, s: (i, g)),
        scratch_shapes=_flash_scratch(t, tq, tq, NSA_GROUP, HEAD_DIM, 3, WINDOW // tq + 1))
    return pl.pallas_call(
        functools.partial(_nsa_win_kernel, tq=tq, t_total=t),
        grid_spec=grid_spec,
        out_shape=jax.ShapeDtypeStruct((t, NSA_DIM), MXU_DTYPE),
        compiler_params=_cparams(("arbitrary", "arbitrary")),
        name="nsa_window_merge",
    )(_nsa_slopes(), u, u, u, o_cmp, o_slc, gl, q_norm.reshape(1, HEAD_DIM), k_norm.reshape(1, HEAD_DIM))


def _mx(w):
    return w.astype(MXU_DTYPE)


def conv_moba_layer(x, attn_norm, w_in, conv_w, conv_b, conv_norm_g, conv_norm_b, q_norm, k_norm,
                    w_out, ffn_norm, w_gate, w_up, w_down):
    u = norm_matmul(x, attn_norm, _mx(w_in))
    a = conv_module(u, conv_w, conv_b, conv_norm_g, conv_norm_b)
    o = moba_attention(u, q_norm, k_norm, col0=2 * CONV_CH // HEAD_DIM)
    w_out = _mx(w_out)
    x = proj_residual(x, [(a, w_out[:CONV_CH]), (o, w_out[CONV_CH:])])
    return dense_ffn(x, ffn_norm, _mx(w_gate), _mx(w_up), _mx(w_down))


def nsa_moe_layer(x, attn_norm, w_in, q_norm, kc_norm, ks_norm, kw_norm, cmp_pe_k, cmp_w1_k, cmp_w2_k,
                  cmp_pe_v, cmp_w1_v, cmp_w2_v, w_out, ffn_norm, w_router, w_gate, w_up, w_down):
    t = x.shape[0]
    main = NSA_DIM + 6 * NSA_KV_DIM
    n_gate = 3 * NSA_HEADS
    gate_tile = 512
    w_in_p = jnp.pad(_mx(w_in), ((0, 0), (0, gate_tile - n_gate)))
    u = norm_matmul(x, attn_norm, w_in_p, tn=gate_tile)
    gate_logits = u[:, main:main + n_gate]
    nrows = t // CMP_STRIDE

    def blocks(col):
        v = u[:, col:col + NSA_KV_DIM].reshape(nrows, CMP_STRIDE, NSA_KV_HEADS, HEAD_DIM)
        return v.transpose(2, 0, 1, 3).reshape(NSA_KV_HEADS, nrows, CMP_STRIDE * HEAD_DIM)

    k_cmp = compress(blocks(NSA_DIM), cmp_pe_k, _mx(cmp_w1_k), _mx(cmp_w2_k), kc_norm, normalize=True)
    v_cmp = compress(blocks(NSA_DIM + NSA_KV_DIM), cmp_pe_v, _mx(cmp_w1_v), _mx(cmp_w2_v), kc_norm,
                     normalize=False)
    o_cmp, sel, blocks_used = nsa_compressed(u, k_cmp, v_cmp, q_norm)
    cb = lambda col: col // HEAD_DIM
    o_slc = nsa_selected(u, sel, blocks_used, q_norm, ks_norm, kcol=cb(NSA_DIM + 2 * NSA_KV_DIM),
                         vcol=cb(NSA_DIM + 3 * NSA_KV_DIM))
    o = nsa_window_merge(u, o_cmp, o_slc, gate_logits, q_norm, kw_norm,
                         kcol=cb(NSA_DIM + 4 * NSA_KV_DIM), vcol=cb(NSA_DIM + 5 * NSA_KV_DIM))
    x = proj_residual(x, [(o, _mx(w_out))])
    return moe_ffn(x, ffn_norm, w_router, _mx(w_gate), _mx(w_up), _mx(w_down))


def kernel(x, l0_attn_norm, l0_w_in, l0_conv_w, l0_conv_b, l0_conv_norm_g, l0_conv_norm_b, l0_q_norm, l0_k_norm, l0_w_out, l0_ffn_norm, l0_w_gate, l0_w_up, l0_w_down, l1_attn_norm, l1_w_in, l1_q_norm, l1_kc_norm, l1_ks_norm, l1_kw_norm, l1_cmp_pe_k, l1_cmp_w1_k, l1_cmp_w2_k, l1_cmp_pe_v, l1_cmp_w1_v, l1_cmp_w2_v, l1_w_out, l1_ffn_norm, l1_w_router, l1_w_gate, l1_w_up, l1_w_down, l2_attn_norm, l2_w_in, l2_conv_w, l2_conv_b, l2_conv_norm_g, l2_conv_norm_b, l2_q_norm, l2_k_norm, l2_w_out, l2_ffn_norm, l2_w_gate, l2_w_up, l2_w_down, l3_attn_norm, l3_w_in, l3_q_norm, l3_kc_norm, l3_ks_norm, l3_kw_norm, l3_cmp_pe_k, l3_cmp_w1_k, l3_cmp_w2_k, l3_cmp_pe_v, l3_cmp_w1_v, l3_cmp_w2_v, l3_w_out, l3_ffn_norm, l3_w_router, l3_w_gate, l3_w_up, l3_w_down):
    b, t, d = x.shape
    assert b == 1 and d == D_MODEL
    h = x.reshape(t, d)
    h = conv_moba_layer(h, l0_attn_norm, l0_w_in, l0_conv_w, l0_conv_b, l0_conv_norm_g, l0_conv_norm_b, l0_q_norm, l0_k_norm, l0_w_out, l0_ffn_norm, l0_w_gate, l0_w_up, l0_w_down)
    h = nsa_moe_layer(h, l1_attn_norm, l1_w_in, l1_q_norm, l1_kc_norm, l1_ks_norm, l1_kw_norm, l1_cmp_pe_k, l1_cmp_w1_k, l1_cmp_w2_k, l1_cmp_pe_v, l1_cmp_w1_v, l1_cmp_w2_v, l1_w_out, l1_ffn_norm, l1_w_router, l1_w_gate, l1_w_up, l1_w_down)
    h = conv_moba_layer(h, l2_attn_norm, l2_w_in, l2_conv_w, l2_conv_b, l2_conv_norm_g, l2_conv_norm_b, l2_q_norm, l2_k_norm, l2_w_out, l2_ffn_norm, l2_w_gate, l2_w_up, l2_w_down)
    h = nsa_moe_layer(h, l3_attn_norm, l3_w_in, l3_q_norm, l3_kc_norm, l3_ks_norm, l3_kw_norm, l3_cmp_pe_k, l3_cmp_w1_k, l3_cmp_w2_k, l3_cmp_pe_v, l3_cmp_w1_v, l3_cmp_w2_v, l3_w_out, l3_ffn_norm, l3_w_router, l3_w_gate, l3_w_up, l3_w_down)
    return h.reshape(b, t, d)
```
